```python
import math
import jax, jax.numpy as jnp
from jax import lax
import numpy as np

D_MODEL = 1024
BATCH = 2
SEQ = 8192
DEPTH = 2

N_MIXERS = 2
EXPAND = 2
E_WIDTH = EXPAND * D_MODEL
HEAD_DIM = 128
N_SLOTS = E_WIDTH // HEAD_DIM
DILATED_GROUPS = ((128, 1), (512, 4), (2048, 16))
N_GROUPS = 3
ROT_DIM = HEAD_DIM // 4
ROPE_THETA = 500000.0
BLOCK = 128
POOL_WINDOWS = (2, 4, 8, 16)
N_POOL = 4
POOL_CH = E_WIDTH // N_POOL
RMS_EPS = 1e-6
NEG_INF = -1e30
N_ATTN_LAYERS = (DEPTH + 1) // 2
N_POOL_LAYERS = DEPTH // 2

kernel_name = "hybrid_dilated_attn_multiscale_pool"


def rmsnorm(x, g):
    x32 = x.astype(jnp.float32)
    y = x32 * lax.rsqrt(jnp.mean(x32 * x32, axis=-1, keepdims=True) + RMS_EPS)
    return (y * g.astype(jnp.float32)).astype(x.dtype)


def rope_partial(t, cos, sin):
    t32 = t.astype(jnp.float32)
    half = ROT_DIM // 2
    t1 = t32[..., :half]
    t2 = t32[..., half:ROT_DIM]
    out = jnp.concatenate([t1 * cos - t2 * sin, t2 * cos + t1 * sin, t32[..., ROT_DIM:]], axis=-1)
    return out.astype(t.dtype)


def dilated_window_attention(q, k, v, dilation, w_sub):
    B, S, H, Dh = q.shape
    L = S // dilation
    nb = -(-L // BLOCK)
    Lp = nb * BLOCK
    N = B * dilation

    def to_sub(t):
        t = t.reshape(B, L, dilation, H, Dh).transpose(0, 2, 1, 3, 4).reshape(N, L, H, Dh)
        return jnp.pad(t, ((0, 0), (0, Lp - L), (0, 0), (0, 0)))

    def band_keys(t):
        tp = jnp.pad(t, ((0, 0), (BLOCK, 0), (0, 0), (0, 0))).reshape(N, nb + 1, BLOCK, H, Dh)
        return jnp.concatenate([tp[:, :-1], tp[:, 1:]], axis=2)

    qs = to_sub(q).reshape(N, nb, BLOCK, H, Dh)
    kw = band_keys(to_sub(k))
    vw = band_keys(to_sub(v))

    qi = jnp.arange(BLOCK)[:, None]
    kj = jnp.arange(2 * BLOCK)[None, :]
    dist = qi + BLOCK - kj
    band = (dist >= 0) & (dist <= w_sub)
    not_first = (jnp.arange(nb) > 0)[:, None, None]
    valid = band[None] & (not_first | (kj >= BLOCK)[None])

    scale = 1.0 / math.sqrt(Dh)
    s = jnp.einsum('nbqhd,nbkhd->nbhqk', qs, kw).astype(jnp.float32) * scale
    s = jnp.where(valid[None, :, None], s, NEG_INF)
    m = jnp.max(s, axis=-1, keepdims=True)
    p = jnp.exp(s - m)
    den = jnp.sum(p, axis=-1)
    o = jnp.einsum('nbhqk,nbkhd->nbqhd', p, vw.astype(jnp.float32))
    o = o / jnp.transpose(den, (0, 1, 3, 2))[..., None]
    lse = jnp.transpose(m[..., 0] + jnp.log(den), (0, 1, 3, 2))

    o = o.reshape(N, Lp, H, Dh)[:, :L].reshape(B, dilation, L, H, Dh)
    o = o.transpose(0, 2, 1, 3, 4).reshape(B, S, H, Dh)
    lse = lse.reshape(N, Lp, H)[:, :L].reshape(B, dilation, L, H)
    lse = lse.transpose(0, 2, 1, 3).reshape(B, S, H)
    return o, lse


def attention_mixer(xn, cos, sin, w_in, w_out):
    B, S, D = xn.shape
    qkv_cols = N_GROUPS * 3 * E_WIDTH
    w_qkv = w_in[:, :qkv_cols].reshape(D, N_GROUPS, 3, N_SLOTS, HEAD_DIM)
    z = xn @ w_in[:, qkv_cols:]
    outs, lses = [], []
    for g, (window, dil) in enumerate(DILATED_GROUPS):
        qkv = jnp.einsum('bsd,dchk->cbshk', xn, w_qkv[:, g])
        q = rope_partial(qkv[0], cos, sin)
        k = rope_partial(qkv[1], cos, sin)
        o, lse = dilated_window_attention(q, k, qkv[2], dil, window // dil)
        outs.append(o)
        lses.append(lse)
    wts = jax.nn.softmax(jnp.stack(lses, axis=0), axis=0)
    y = jnp.sum(wts[..., None] * jnp.stack(outs, axis=0), axis=0)
    y = y.reshape(B, S, E_WIDTH).astype(xn.dtype) * jax.nn.silu(z)
    return y @ w_out


def pooling_mixer(xn, w_in, w_grp, b_grp, scale, w_out):
    B, S, D = xn.shape
    uz = xn @ w_in
    u, z = uz[..., :E_WIDTH], uz[..., E_WIDTH:]
    ug = u.reshape(B, S, N_POOL, POOL_CH).astype(jnp.float32)
    c = jnp.cumsum(ug, axis=1)
    t = jnp.arange(S)
    parts = []
    for g, w in enumerate(POOL_WINDOWS):
        cg = c[:, :, g]
        lag = jnp.pad(cg[:, :S - w], ((0, 0), (w, 0), (0, 0)))
        cnt = jnp.minimum(t + 1, w).astype(jnp.float32)[None, :, None]
        parts.append((cg - lag) / cnt)
    pooled = (jnp.stack(parts, axis=2) - ug).astype(xn.dtype)
    h = jnp.einsum('bsgc,gcd->bsgd', pooled, w_grp) + b_grp
    h = h.reshape(B, S, E_WIDTH) * scale
    y = h * jax.nn.silu(z)
    return y @ w_out


def setup_inputs(seed: int = 0) -> dict:
    key = jax.random.key(seed)
    ks = jax.random.split(key, 12)
    D, E = D_MODEL, E_WIDTH
    f32 = jnp.float32
    x = jax.random.normal(ks[0], (BATCH, SEQ, D), f32)
    offset = jax.random.randint(ks[1], (BATCH, 1), 0, 4096, dtype=jnp.int32)
    positions = (offset + jnp.arange(SEQ, dtype=jnp.int32)[None, :]).astype(jnp.int32)
    norm_pre = 1.0 + 0.1 * jax.random.normal(ks[2], (DEPTH, D), f32)
    norm_post = 1.0 + 0.1 * jax.random.normal(ks[3], (DEPTH, D), f32)
    attn_w_in = jax.random.normal(ks[4], (N_ATTN_LAYERS, D, N_GROUPS * 3 * E + E), f32) * D ** -0.5
    attn_w_out = jax.random.normal(ks[5], (N_ATTN_LAYERS, E, D), f32) * E ** -0.5
    pool_w_in = jax.random.normal(ks[6], (N_POOL_LAYERS, D, 2 * E), f32) * D ** -0.5
    pool_w_grp = jax.random.normal(ks[7], (N_POOL_LAYERS, N_POOL, POOL_CH, POOL_CH), f32) * POOL_CH ** -0.5
    pool_b_grp = 0.01 * jax.random.normal(ks[8], (N_POOL_LAYERS, N_POOL, POOL_CH), f32)
    pool_scale = 1.0 + 0.1 * jax.random.normal(ks[9], (N_POOL_LAYERS, E), f32)
    pool_w_out = jax.random.normal(ks[10], (N_POOL_LAYERS, E, D), f32) * E ** -0.5
    return {"x": x, "positions": positions, "norm_pre": norm_pre, "norm_post": norm_post,
            "attn_w_in": attn_w_in, "attn_w_out": attn_w_out,
            "pool_w_in": pool_w_in, "pool_w_grp": pool_w_grp, "pool_b_grp": pool_b_grp,
            "pool_scale": pool_scale, "pool_w_out": pool_w_out}


def reference(x, positions, norm_pre, norm_post, attn_w_in, attn_w_out,
              pool_w_in, pool_w_grp, pool_b_grp, pool_scale, pool_w_out):
    inv_freq = ROPE_THETA ** (-jnp.arange(0, ROT_DIM, 2, dtype=jnp.float32) / ROT_DIM)
    ang = positions.astype(jnp.float32)[..., None] * inv_freq
    cos = jnp.cos(ang)[:, :, None, :]
    sin = jnp.sin(ang)[:, :, None, :]
    h = x
    for i in range(DEPTH):
        xn = rmsnorm(h, norm_pre[i])
        j = i // N_MIXERS
        if i % N_MIXERS == 0:
            y = attention_mixer(xn, cos, sin, attn_w_in[j], attn_w_out[j])
        else:
            y = pooling_mixer(xn, pool_w_in[j], pool_w_grp[j], pool_b_grp[j],
                              pool_scale[j], pool_w_out[j])
        h = h + rmsnorm(y, norm_post[i])
    return h
```

```python
import functools
import math

import numpy as np
import jax
import jax.numpy as jnp
from jax import lax
from jax.experimental import pallas as pl
from jax.experimental.pallas import tpu as pltpu

F32 = jnp.float32
BF16 = jnp.bfloat16

D_MODEL = 1024
E_WIDTH = 2048
HEAD_DIM = 128
N_HEADS = 16
ROT_DIM = 32
ROPE_THETA = 500000.0
RMS_EPS = 1e-6
NEG_INF = -1e30
POOL_WINDOWS = (2, 4, 8, 16)
POOL_CH = 512

RES = 16
MM = 128
CHUNK = RES * MM
QBLK = 128
HALO = (8, 32, 128)
PITCH = tuple(h + MM for h in HALO)
SCALE = 1.0 / math.sqrt(HEAD_DIM)

VMEM_LIMIT_V7X = 60000 * 1024


def _rms(x, g):
    ms = jnp.mean(x * x, axis=-1, keepdims=True)
    return x * lax.rsqrt(ms + RMS_EPS) * g


def _silu(z):
    return z * (1.0 / (1.0 + jnp.exp(-z)))


def _prenorm_kernel(x_ref, pos_ref, g_ref, invf_ref, xn_ref, rope_ref):
    g = g_ref[...]
    invf = invf_ref[...]
    lane = lax.broadcasted_iota(jnp.int32, (1, HEAD_DIM), 1)
    lo = (lane < ROT_DIM // 2).astype(F32)
    hi = ((lane >= ROT_DIM // 2) & (lane < ROT_DIM)).astype(F32)
    for r in range(RES):
        x = x_ref[:, r * D_MODEL:(r + 1) * D_MODEL]
        xn_ref[r * MM:(r + 1) * MM, :] = _rms(x, g).astype(BF16)
        ang = pos_ref[:, r:r + 1].astype(F32) * invf
        sn = jnp.sin(ang)
        rope_ref[r * MM:(r + 1) * MM, 0:128] = jnp.cos(ang)
        rope_ref[r * MM:(r + 1) * MM, 128:256] = -sn * lo
        rope_ref[r * MM:(r + 1) * MM, 256:384] = sn * hi


def _prenorm(x3, pos3, g, invf):
    b = x3.shape[0]
    n_chunks = x3.shape[1] // MM
    seq = n_chunks * CHUNK
    return pl.pallas_call(
        _prenorm_kernel,
        grid=(b, n_chunks),
        in_specs=[
            pl.BlockSpec((None, MM, RES * D_MODEL), lambda i, c: (i, c, 0)),
            pl.BlockSpec((None, MM, RES), lambda i, c: (i, c, 0)),
            pl.BlockSpec((1, D_MODEL), lambda i, c: (0, 0)),
            pl.BlockSpec((1, HEAD_DIM), lambda i, c: (0, 0)),
        ],
        out_specs=[
            pl.BlockSpec((None, CHUNK, D_MODEL), lambda i, c: (i, c, 0)),
            pl.BlockSpec((None, CHUNK, 3 * HEAD_DIM), lambda i, c: (i, c, 0)),
        ],
        out_shape=[
            jax.ShapeDtypeStruct((b, seq, D_MODEL), BF16),
            jax.ShapeDtypeStruct((b, seq, 3 * HEAD_DIM), F32),
        ],
        compiler_params=pltpu.CompilerParams(
            dimension_semantics=("arbitrary", "arbitrary"),
            vmem_limit_bytes=VMEM_LIMIT_V7X),
        name="l0_prenorm",
    )(x3, pos3, g, invf)


def _band_bias():
    qi = np.arange(QBLK)[:, None]
    kj = np.arange(2 * QBLK)[None, :]
    out = np.zeros((3, 2, QBLK, 2 * QBLK), np.float32)
    dist0 = 16 * ((qi % 8) - (kj % 16) + 8) + (qi // 8) - (kj // 16)
    halo0 = (kj % 16) < 8
    dist1 = 4 * ((qi % 32) - (kj % 64) + 32) + (qi // 32) - (kj // 64)
    halo1 = (kj % 64) < 32
    dist2 = qi - kj + 128
    halo2 = kj < 128
    for g, (dist, halo) in enumerate(((dist0, halo0), (dist1, halo1), (dist2, halo2))):
        band = (dist >= 0) & (dist <= QBLK)
        out[g, 0] = np.where(band, 0.0, NEG_INF)
        out[g, 1] = np.where(band & ~halo, 0.0, NEG_INF)
    return jnp.asarray(out)


def _attn_block(q, k, v, bias):
    s = lax.dot_general(q, k, (((1,), (1,)), ((), ())), preferred_element_type=F32) + bias
    m = jnp.max(s, axis=-1, keepdims=True)
    p = jnp.exp(s - m)
    den = jnp.sum(p, axis=-1, keepdims=True)
    o = jnp.dot(p.astype(BF16), v, preferred_element_type=F32)
    o = o * (1.0 / den)
    lse = m + jnp.log(den)
    return o, jnp.broadcast_to(lse, (QBLK, HEAD_DIM))


def _attn_kernel(xn_ref, rope_ref, wq0, wk0, wv0, wz, wq1, wk1, wq2, wk2, wv1, wv2, bias_ref,
                 y_ref,
                 w_s, q0_s, k0_s, v0_s, q1_s, k1_s, v1_s, q2_s, k2_s, v2_s, z_s, o_s, lse_s):
    c = pl.program_id(2)
    kv_bufs = ((k0_s, v0_s), (k1_s, v1_s), (k2_s, v2_s))

    @pl.when(c == 0)
    def _():
        for j, w in enumerate((wq0, wk0, wv0, wz, wq1, wk1, wq2, wk2, wv1, wv2)):
            w_s[:, j * HEAD_DIM:(j + 1) * HEAD_DIM] = w[...].astype(BF16)
        for kb, vb in kv_bufs:
            kb[...] = jnp.zeros(kb.shape, kb.dtype)
            vb[...] = jnp.zeros(vb.shape, vb.dtype)

    x = xn_ref[...]
    cos = rope_ref[:, 0:128]
    sin_a = rope_ref[:, 128:256]
    sin_b = rope_ref[:, 256:384]

    def rope(t):
        return t * cos + pltpu.roll(t, HEAD_DIM - ROT_DIM // 2, 1) * sin_a + pltpu.roll(t, ROT_DIM // 2, 1) * sin_b

    def proj(tile):
        return jnp.dot(x, w_s[:, tile * 256:(tile + 1) * 256], preferred_element_type=F32)

    def put_cur(buf, g, val):
        for r in range(RES):
            buf[r * PITCH[g] + HALO[g]:(r + 1) * PITCH[g], :] = val[r * MM:(r + 1) * MM].astype(buf.dtype)

    t = proj(0)
    q0_s[...] = rope(t[:, :128]) * SCALE
    put_cur(k0_s, 0, rope(t[:, 128:]))
    t = proj(1)
    put_cur(v0_s, 0, t[:, :128])
    z_s[...] = t[:, 128:]
    t = proj(2)
    q1_s[...] = (rope(t[:, :128]) * SCALE).astype(BF16)
    put_cur(k1_s, 1, rope(t[:, 128:]))
    t = proj(3)
    q2_s[...] = (rope(t[:, :128]) * SCALE).astype(BF16)
    put_cur(k2_s, 2, rope(t[:, 128:]))
    t = proj(4)
    put_cur(v1_s, 1, t[:, :128])
    put_cur(v2_s, 2, t[:, 128:])

    first = c == 0

    def body0(qb, carry):
        off = qb * 8
        q = jnp.concatenate(
            [q0_s[pl.ds(pl.multiple_of(r * MM + off, 8), 8), :] for r in range(RES)], axis=0).astype(BF16)
        k = jnp.concatenate(
            [k0_s[pl.ds(pl.multiple_of(r * PITCH[0] + off, 8), 16), :] for r in range(RES)], axis=0).astype(BF16)
        v = jnp.concatenate(
            [v0_s[pl.ds(pl.multiple_of(r * PITCH[0] + off, 8), 16), :] for r in range(RES)], axis=0).astype(BF16)
        flag = jnp.logical_and(first, qb == 0).astype(jnp.int32)
        o, lse = _attn_block(q, k, v, bias_ref[0, flag])
        for r in range(RES):
            rows = pl.ds(pl.multiple_of(r * MM + off, 8), 8)
            o_s[0, rows, :] = o[r * 8:(r + 1) * 8]
            lse_s[0, rows, :] = lse[r * 8:(r + 1) * 8]
        return carry

    lax.fori_loop(0, MM // 8, body0, 0)

    def body1(it, carry):
        r4 = it // 4
        off = (it % 4) * 32
        q = jnp.concatenate(
            [q1_s[pl.ds(pl.multiple_of((r4 + 4 * cc) * MM + off, 32), 32), :] for cc in range(4)], axis=0)
        k = jnp.concatenate(
            [k1_s[pl.ds(pl.multiple_of((r4 + 4 * cc) * PITCH[1] + off, 32), 64), :] for cc in range(4)], axis=0)
        v = jnp.concatenate(
            [v1_s[pl.ds(pl.multiple_of((r4 + 4 * cc) * PITCH[1] + off, 32), 64), :] for cc in range(4)], axis=0)
        flag = jnp.logical_and(first, (it % 4) == 0).astype(jnp.int32)
        o, lse = _attn_block(q, k, v, bias_ref[1, flag])
        for cc in range(4):
            rows = pl.ds(pl.multiple_of((r4 + 4 * cc) * MM + off, 32), 32)
            o_s[1, rows, :] = o[cc * 32:(cc + 1) * 32]
            lse_s[1, rows, :] = lse[cc * 32:(cc + 1) * 32]
        return carry

    lax.fori_loop(0, 16, body1, 0)

    def body2(r, carry):
        qrows = pl.ds(pl.multiple_of(r * MM, MM), MM)
        krows = pl.ds(pl.multiple_of(r * PITCH[2], PITCH[2]), PITCH[2])
        flag = first.astype(jnp.int32)
        o, lse = _attn_block(q2_s[qrows, :], k2_s[krows, :], v2_s[krows, :], bias_ref[2, flag])
        o_s[2, qrows, :] = o
        lse_s[2, qrows, :] = lse
        return carry

    lax.fori_loop(0, RES, body2, 0)

    for g, (kb, vb) in enumerate(kv_bufs):
        for r in range(RES):
            base = r * PITCH[g]
            kb[base:base + HALO[g], :] = kb[base + MM:base + MM + HALO[g], :]
            vb[base:base + HALO[g], :] = vb[base + MM:base + MM + HALO[g], :]

    def comb(i, carry):
        rows = pl.ds(pl.multiple_of(i * 256, 256), 256)
        l0 = lse_s[0, rows, :]
        l1 = lse_s[1, rows, :]
        l2 = lse_s[2, rows, :]
        lmax = jnp.maximum(jnp.maximum(l0, l1), l2)
        w0 = jnp.exp(l0 - lmax)
        w1 = jnp.exp(l1 - lmax)
        w2 = jnp.exp(l2 - lmax)
        y = (w0 * o_s[0, rows, :] + w1 * o_s[1, rows, :] + w2 * o_s[2, rows, :]) / (w0 + w1 + w2)
        y_ref[rows, :] = (y * _silu(z_s[rows, :])).astype(BF16)
        return carry

    lax.fori_loop(0, CHUNK // 256, comb, 0)


def _attention(xn, rope, w_in, bias):
    b, seq, _ = xn.shape
    n_chunks = seq // CHUNK

    def wspec(col_block):
        return pl.BlockSpec((D_MODEL, HEAD_DIM), lambda i, h, c: (0, col_block + h))

    col = lambda g, j: (g * 3 + j) * N_HEADS
    w_specs = [wspec(col(0, 0)), wspec(col(0, 1)), wspec(col(0, 2)), wspec(9 * N_HEADS),
               wspec(col(1, 0)), wspec(col(1, 1)), wspec(col(2, 0)), wspec(col(2, 1)),
               wspec(col(1, 2)), wspec(col(2, 2))]
    kv_rows = [RES * p for p in PITCH]
    return pl.pallas_call(
        _attn_kernel,
        grid=(b, N_HEADS, n_chunks),
        in_specs=[
            pl.BlockSpec((None, CHUNK, D_MODEL), lambda i, h, c: (i, c, 0)),
            pl.BlockSpec((None, CHUNK, 3 * HEAD_DIM), lambda i, h, c: (i, c, 0)),
            *w_specs,
            pl.BlockSpec((3, 2, QBLK, 2 * QBLK), lambda i, h, c: (0, 0, 0, 0)),
        ],
        out_specs=pl.BlockSpec((None, CHUNK, HEAD_DIM), lambda i, h, c: (i, c, h)),
        out_shape=jax.ShapeDtypeStruct((b, seq, E_WIDTH), BF16),
        scratch_shapes=[
            pltpu.VMEM((D_MODEL, 10 * HEAD_DIM), BF16),
            pltpu.VMEM((CHUNK, HEAD_DIM), F32),
            pltpu.VMEM((kv_rows[0], HEAD_DIM), F32),
            pltpu.VMEM((kv_rows[0], HEAD_DIM), F32),
            pltpu.VMEM((CHUNK, HEAD_DIM), BF16),
            pltpu.VMEM((kv_rows[1], HEAD_DIM), BF16),
            pltpu.VMEM((kv_rows[1], HEAD_DIM), BF16),
            pltpu.VMEM((CHUNK, HEAD_DIM), BF16),
            pltpu.VMEM((kv_rows[2], HEAD_DIM), BF16),
            pltpu.VMEM((kv_rows[2], HEAD_DIM), BF16),
            pltpu.VMEM((CHUNK, HEAD_DIM), F32),
            pltpu.VMEM((3, CHUNK, HEAD_DIM), F32),
            pltpu.VMEM((3, CHUNK, HEAD_DIM), F32),
        ],
        compiler_params=pltpu.CompilerParams(
            dimension_semantics=("arbitrary", "arbitrary", "arbitrary"),
            vmem_limit_bytes=VMEM_LIMIT_V7X),
        name="l0_attention",
    )(xn, rope, *([w_in] * 10), bias)


OUT_RES = 8


def _outproj_kernel(y_ref, w_ref, x_ref, g_ref, o_ref):
    p = jnp.dot(y_ref[...], w_ref[...], preferred_element_type=F32)
    g = g_ref[...]
    for i in range(OUT_RES):
        cols = slice(i * D_MODEL, (i + 1) * D_MODEL)
        o_ref[:, cols] = x_ref[:, cols] + _rms(p[i * MM:(i + 1) * MM], g)


def _outproj(y, w_out, x3, g):
    b, seq, _ = y.shape
    n_chunks = seq // CHUNK
    steps = RES // OUT_RES
    return pl.pallas_call(
        _outproj_kernel,
        grid=(b, n_chunks, steps),
        in_specs=[
            pl.BlockSpec((None, OUT_RES * MM, E_WIDTH), lambda i, c, j: (i, c * steps + j, 0)),
            pl.BlockSpec((E_WIDTH, D_MODEL), lambda i, c, j: (0, 0)),
            pl.BlockSpec((None, MM, OUT_RES * D_MODEL), lambda i, c, j: (i, c, j)),
            pl.BlockSpec((1, D_MODEL), lambda i, c, j: (0, 0)),
        ],
        out_specs=pl.BlockSpec((None, MM, OUT_RES * D_MODEL), lambda i, c, j: (i, c, j)),
        out_shape=jax.ShapeDtypeStruct(x3.shape, F32),
        compiler_params=pltpu.CompilerParams(
            dimension_semantics=("arbitrary", "arbitrary", "arbitrary"),
            vmem_limit_bytes=VMEM_LIMIT_V7X),
        name="l0_outproj",
    )(y, w_out, x3, g)


POOL_TM = 512
POOL_HALO = 16


def _pool_kernel(h_ref, gpre_ref, win_ref, wgrp_ref, bgrp_ref, scale_ref, wout_ref, gpost_ref,
                 o_ref, carry_s, y_s):
    t = pl.program_id(1)

    @pl.when(t == 0)
    def _():
        carry_s[...] = jnp.zeros(carry_s.shape, carry_s.dtype)

    h = h_ref[...]
    xn = _rms(h, gpre_ref[...]).astype(BF16)
    pos = lax.broadcasted_iota(jnp.int32, (POOL_TM, 1), 0) + t * POOL_TM
    for g, w in enumerate(POOL_WINDOWS):
        cols = slice(g * POOL_CH, (g + 1) * POOL_CH)
        u = jnp.dot(xn, win_ref[:, cols], preferred_element_type=F32)
        z = jnp.dot(xn, win_ref[:, E_WIDTH + g * POOL_CH:E_WIDTH + (g + 1) * POOL_CH],
                    preferred_element_type=F32)
        s = jnp.concatenate([carry_s[:, cols], u], axis=0)
        k = 1
        while k < w:
            s = s + pltpu.roll(s, k, 0)
            k *= 2
        carry_s[:, cols] = u[POOL_TM - POOL_HALO:]
        inv_cnt = 1.0 / jnp.minimum(pos + 1, w).astype(F32)
        pooled = s[POOL_HALO:] * inv_cnt - u
        hg = jnp.dot(pooled.astype(BF16), wgrp_ref[g], preferred_element_type=F32) + bgrp_ref[:, cols]
        y_s[:, cols] = (hg * scale_ref[:, cols] * _silu(z)).astype(BF16)
    p = jnp.dot(y_s[...], wout_ref[...], preferred_element_type=F32)
    o_ref[...] = h + _rms(p, gpost_ref[...])


def _pool_layer(h, gpre, w_in, w_grp, b_grp, scale, w_out, gpost):
    b, seq, _ = h.shape
    const2 = lambda i, t: (0, 0)
    return pl.pallas_call(
        _pool_kernel,
        grid=(b, seq // POOL_TM),
        in_specs=[
            pl.BlockSpec((None, POOL_TM, D_MODEL), lambda i, t: (i, t, 0)),
            pl.BlockSpec((1, D_MODEL), const2),
            pl.BlockSpec((D_MODEL, 2 * E_WIDTH), const2),
            pl.BlockSpec((len(POOL_WINDOWS), POOL_CH, POOL_CH), lambda i, t: (0, 0, 0)),
            pl.BlockSpec((1, E_WIDTH), const2),
            pl.BlockSpec((1, E_WIDTH), const2),
            pl.BlockSpec((E_WIDTH, D_MODEL), const2),
            pl.BlockSpec((1, D_MODEL), const2),
        ],
        out_specs=pl.BlockSpec((None, POOL_TM, D_MODEL), lambda i, t: (i, t, 0)),
        out_shape=jax.ShapeDtypeStruct(h.shape, F32),
        scratch_shapes=[
            pltpu.VMEM((POOL_HALO, E_WIDTH), F32),
            pltpu.VMEM((POOL_TM, E_WIDTH), BF16),
        ],
        compiler_params=pltpu.CompilerParams(
            dimension_semantics=("arbitrary", "arbitrary"),
            vmem_limit_bytes=VMEM_LIMIT_V7X),
        name="l1_pool",
    )(h, gpre, w_in, w_grp, b_grp, scale, w_out, gpost)


def kernel(x, positions, norm_pre, norm_post, attn_w_in, attn_w_out,
           pool_w_in, pool_w_grp, pool_b_grp, pool_scale, pool_w_out):
    b, seq, d = x.shape
    assert d == D_MODEL and seq % CHUNK == 0
    assert attn_w_in.shape[0] == 1 and pool_w_in.shape[0] == 1
    inv_freq = ROPE_THETA ** (-jnp.arange(0, ROT_DIM, 2, dtype=F32) / ROT_DIM)
    invf = jnp.concatenate([inv_freq, inv_freq, jnp.zeros((HEAD_DIM - ROT_DIM,), F32)])[None, :]

    x3 = x.reshape(b, seq // RES, RES * D_MODEL)
    pos3 = positions.reshape(b, seq // RES, RES)
    xn, rope = _prenorm(x3, pos3, norm_pre[0:1], invf)
    y = _attention(xn, rope, attn_w_in[0], _band_bias())
    h1 = _outproj(y, attn_w_out[0].astype(BF16), x3, norm_post[0:1]).reshape(b, seq, d)

    return _pool_layer(
        h1, norm_pre[1:2], pool_w_in[0].astype(BF16), pool_w_grp[0].astype(BF16),
        pool_b_grp[0].reshape(1, E_WIDTH), pool_scale[0:1], pool_w_out[0].astype(BF16), norm_post[1:2])
```

```python
import functools
import math

import numpy as np
import jax
import jax.numpy as jnp
from jax import lax
from jax.experimental import pallas as pl
from jax.experimental.pallas import tpu as pltpu

F32 = jnp.float32
BF16 = jnp.bfloat16

D_MODEL = 1024
E_WIDTH = 2048
HEAD_DIM = 128
N_HEADS = 16
ROT_DIM = 32
ROPE_THETA = 500000.0
RMS_EPS = 1e-6
NEG_INF = -1e30
POOL_WINDOWS = (2, 4, 8, 16)
POOL_CH = 512

RES = 16
MM = 128
CHUNK = RES * MM
QBLK = 128
HALO = (8, 32, 128)
PITCH = tuple(h + MM for h in HALO)
SCALE = 1.0 / math.sqrt(HEAD_DIM)

VMEM_LIMIT_V7X = 60000 * 1024
ATTN_UNROLL = 8


def _rms(x, g):
    ms = jnp.mean(x * x, axis=-1, keepdims=True)
    return x * lax.rsqrt(ms + RMS_EPS) * g


def _silu(z):
    return z * (1.0 / (1.0 + jnp.exp(-z)))


def _prenorm_kernel(x_ref, pos_ref, g_ref, invf_ref, xn_ref, rope_ref):
    g = g_ref[...]
    invf = invf_ref[...]
    lane = lax.broadcasted_iota(jnp.int32, (1, HEAD_DIM), 1)
    lo = (lane < ROT_DIM // 2).astype(F32)
    hi = ((lane >= ROT_DIM // 2) & (lane < ROT_DIM)).astype(F32)
    for r in range(RES):
        x = x_ref[:, r * D_MODEL:(r + 1) * D_MODEL]
        xn_ref[r * MM:(r + 1) * MM, :] = _rms(x, g).astype(BF16)
        ang = pos_ref[:, r:r + 1].astype(F32) * invf
        sn = jnp.sin(ang)
        rope_ref[r * MM:(r + 1) * MM, 0:128] = jnp.cos(ang)
        rope_ref[r * MM:(r + 1) * MM, 128:256] = -sn * lo
        rope_ref[r * MM:(r + 1) * MM, 256:384] = sn * hi


def _prenorm(x3, pos3, g, invf):
    b = x3.shape[0]
    n_chunks = x3.shape[1] // MM
    seq = n_chunks * CHUNK
    return pl.pallas_call(
        _prenorm_kernel,
        grid=(b, n_chunks),
        in_specs=[
            pl.BlockSpec((None, MM, RES * D_MODEL), lambda i, c: (i, c, 0)),
            pl.BlockSpec((None, MM, RES), lambda i, c: (i, c, 0)),
            pl.BlockSpec((1, D_MODEL), lambda i, c: (0, 0)),
            pl.BlockSpec((1, HEAD_DIM), lambda i, c: (0, 0)),
        ],
        out_specs=[
            pl.BlockSpec((None, CHUNK, D_MODEL), lambda i, c: (i, c, 0)),
            pl.BlockSpec((None, CHUNK, 3 * HEAD_DIM), lambda i, c: (i, c, 0)),
        ],
        out_shape=[
            jax.ShapeDtypeStruct((b, seq, D_MODEL), BF16),
            jax.ShapeDtypeStruct((b, seq, 3 * HEAD_DIM), F32),
        ],
        compiler_params=pltpu.CompilerParams(
            dimension_semantics=("arbitrary", "arbitrary"),
            vmem_limit_bytes=VMEM_LIMIT_V7X),
        name="l0_prenorm",
    )(x3, pos3, g, invf)


def _band_bias():
    qi = np.arange(QBLK)[:, None]
    kj = np.arange(2 * QBLK)[None, :]
    out = np.zeros((3, 2, QBLK, 2 * QBLK), np.float32)
    dist0 = 16 * ((qi % 8) - (kj % 16) + 8) + (qi // 8) - (kj // 16)
    halo0 = (kj % 16) < 8
    dist1 = 4 * ((qi % 32) - (kj % 64) + 32) + (qi // 32) - (kj // 64)
    halo1 = (kj % 64) < 32
    dist2 = qi - kj + 128
    halo2 = kj < 128
    for g, (dist, halo) in enumerate(((dist0, halo0), (dist1, halo1), (dist2, halo2))):
        band = (dist >= 0) & (dist <= QBLK)
        out[g, 0] = np.where(band, 0.0, NEG_INF)
        out[g, 1] = np.where(band & ~halo, 0.0, NEG_INF)
    return jnp.asarray(out)


def _attn_block(q, k, v, bias):
    s = lax.dot_general(q, k, (((1,), (1,)), ((), ())), preferred_element_type=F32) + bias
    m = jnp.max(s, axis=-1, keepdims=True)
    p = jnp.exp(s - m)
    den = jnp.sum(p, axis=-1, keepdims=True)
    o = jnp.dot(p.astype(BF16), v, preferred_element_type=F32)
    o = o * (1.0 / den)
    lse = m + jnp.log(den)
    return o, jnp.broadcast_to(lse, (QBLK, HEAD_DIM))


def _attn_kernel(xn_ref, rope_ref, wq0, wk0, wv0, wz, wq1, wk1, wq2, wk2, wv1, wv2, bias_ref,
                 y_ref,
                 w_s, q0_s, k0_s, v0_s, q1_s, k1_s, v1_s, q2_s, k2_s, v2_s, z_s, o_s, lse_s):
    c = pl.program_id(2)
    kv_bufs = ((k0_s, v0_s), (k1_s, v1_s), (k2_s, v2_s))

    @pl.when(c == 0)
    def _():
        for j, w in enumerate((wq0, wk0, wv0, wz, wq1, wk1, wq2, wk2, wv1, wv2)):
            w_s[:, j * HEAD_DIM:(j + 1) * HEAD_DIM] = w[...].astype(BF16)
        for kb, vb in kv_bufs:
            kb[...] = jnp.zeros(kb.shape, kb.dtype)
            vb[...] = jnp.zeros(vb.shape, vb.dtype)

    x = xn_ref[...]
    cos = rope_ref[:, 0:128]
    sin_a = rope_ref[:, 128:256]
    sin_b = rope_ref[:, 256:384]

    def rope(t):
        return t * cos + pltpu.roll(t, HEAD_DIM - ROT_DIM // 2, 1) * sin_a + pltpu.roll(t, ROT_DIM // 2, 1) * sin_b

    def proj(tile):
        return jnp.dot(x, w_s[:, tile * 256:(tile + 1) * 256], preferred_element_type=F32)

    def put_cur(buf, g, val):
        for r in range(RES):
            buf[r * PITCH[g] + HALO[g]:(r + 1) * PITCH[g], :] = val[r * MM:(r + 1) * MM].astype(buf.dtype)

    t = proj(0)
    q0_s[...] = rope(t[:, :128]) * SCALE
    put_cur(k0_s, 0, rope(t[:, 128:]))
    t = proj(1)
    put_cur(v0_s, 0, t[:, :128])
    z_s[...] = t[:, 128:]
    t = proj(2)
    q1_s[...] = (rope(t[:, :128]) * SCALE).astype(BF16)
    put_cur(k1_s, 1, rope(t[:, 128:]))
    t = proj(3)
    q2_s[...] = (rope(t[:, :128]) * SCALE).astype(BF16)
    put_cur(k2_s, 2, rope(t[:, 128:]))
    t = proj(4)
    put_cur(v1_s, 1, t[:, :128])
    put_cur(v2_s, 2, t[:, 128:])

    first = c == 0

    def body0(qb, carry):
        off = qb * 8
        q = jnp.concatenate(
            [q0_s[pl.ds(pl.multiple_of(r * MM + off, 8), 8), :] for r in range(RES)], axis=0).astype(BF16)
        k = jnp.concatenate(
            [k0_s[pl.ds(pl.multiple_of(r * PITCH[0] + off, 8), 16), :] for r in range(RES)], axis=0).astype(BF16)
        v = jnp.concatenate(
            [v0_s[pl.ds(pl.multiple_of(r * PITCH[0] + off, 8), 16), :] for r in range(RES)], axis=0).astype(BF16)
        flag = jnp.logical_and(first, qb == 0).astype(jnp.int32)
        o, lse = _attn_block(q, k, v, bias_ref[0, flag])
        for r in range(RES):
            rows = pl.ds(pl.multiple_of(r * MM + off, 8), 8)
            o_s[0, rows, :] = o[r * 8:(r + 1) * 8]
            lse_s[0, rows, :] = lse[r * 8:(r + 1) * 8]
        return carry


    def body1(it, carry):
        r4 = it // 4
        off = (it % 4) * 32
        q = jnp.concatenate(
            [q1_s[pl.ds(pl.multiple_of((r4 + 4 * cc) * MM + off, 32), 32), :] for cc in range(4)], axis=0)
        k = jnp.concatenate(
            [k1_s[pl.ds(pl.multiple_of((r4 + 4 * cc) * PITCH[1] + off, 32), 64), :] for cc in range(4)], axis=0)
        v = jnp.concatenate(
            [v1_s[pl.ds(pl.multiple_of((r4 + 4 * cc) * PITCH[1] + off, 32), 64), :] for cc in range(4)], axis=0)
        flag = jnp.logical_and(first, (it % 4) == 0).astype(jnp.int32)
        o, lse = _attn_block(q, k, v, bias_ref[1, flag])
        for cc in range(4):
            rows = pl.ds(pl.multiple_of((r4 + 4 * cc) * MM + off, 32), 32)
            o_s[1, rows, :] = o[cc * 32:(cc + 1) * 32]
            lse_s[1, rows, :] = lse[cc * 32:(cc + 1) * 32]
        return carry


    def body2(r, carry):
        qrows = pl.ds(pl.multiple_of(r * MM, MM), MM)
        krows = pl.ds(pl.multiple_of(r * PITCH[2], PITCH[2]), PITCH[2])
        flag = first.astype(jnp.int32)
        o, lse = _attn_block(q2_s[qrows, :], k2_s[krows, :], v2_s[krows, :], bias_ref[2, flag])
        o_s[2, qrows, :] = o
        lse_s[2, qrows, :] = lse
        return carry

    def blocks(i, carry):
        body0(i, carry)
        body1(i, carry)
        body2(i, carry)
        return carry

    lax.fori_loop(0, RES, blocks, 0, unroll=ATTN_UNROLL)

    for g, (kb, vb) in enumerate(kv_bufs):
        for r in range(RES):
            base = r * PITCH[g]
            kb[base:base + HALO[g], :] = kb[base + MM:base + MM + HALO[g], :]
            vb[base:base + HALO[g], :] = vb[base + MM:base + MM + HALO[g], :]

    def comb(i, carry):
        rows = pl.ds(pl.multiple_of(i * 256, 256), 256)
        l0 = lse_s[0, rows, :]
        l1 = lse_s[1, rows, :]
        l2 = lse_s[2, rows, :]
        lmax = jnp.maximum(jnp.maximum(l0, l1), l2)
        w0 = jnp.exp(l0 - lmax)
        w1 = jnp.exp(l1 - lmax)
        w2 = jnp.exp(l2 - lmax)
        y = (w0 * o_s[0, rows, :] + w1 * o_s[1, rows, :] + w2 * o_s[2, rows, :]) / (w0 + w1 + w2)
        y_ref[rows, :] = (y * _silu(z_s[rows, :])).astype(BF16)
        return carry

    lax.fori_loop(0, CHUNK // 256, comb, 0)


def _attention(xn, rope, w_in, bias):
    b, seq, _ = xn.shape
    n_chunks = seq // CHUNK

    def wspec(col_block):
        return pl.BlockSpec((D_MODEL, HEAD_DIM), lambda i, h, c: (0, col_block + h))

    col = lambda g, j: (g * 3 + j) * N_HEADS
    w_specs = [wspec(col(0, 0)), wspec(col(0, 1)), wspec(col(0, 2)), wspec(9 * N_HEADS),
               wspec(col(1, 0)), wspec(col(1, 1)), wspec(col(2, 0)), wspec(col(2, 1)),
               wspec(col(1, 2)), wspec(col(2, 2))]
    kv_rows = [RES * p for p in PITCH]
    return pl.pallas_call(
        _attn_kernel,
        grid=(b, N_HEADS, n_chunks),
        in_specs=[
            pl.BlockSpec((None, CHUNK, D_MODEL), lambda i, h, c: (i, c, 0)),
            pl.BlockSpec((None, CHUNK, 3 * HEAD_DIM), lambda i, h, c: (i, c, 0)),
            *w_specs,
            pl.BlockSpec((3, 2, QBLK, 2 * QBLK), lambda i, h, c: (0, 0, 0, 0)),
        ],
        out_specs=pl.BlockSpec((None, CHUNK, HEAD_DIM), lambda i, h, c: (i, c, h)),
        out_shape=jax.ShapeDtypeStruct((b, seq, E_WIDTH), BF16),
        scratch_shapes=[
            pltpu.VMEM((D_MODEL, 10 * HEAD_DIM), BF16),
            pltpu.VMEM((CHUNK, HEAD_DIM), F32),
            pltpu.VMEM((kv_rows[0], HEAD_DIM), F32),
            pltpu.VMEM((kv_rows[0], HEAD_DIM), F32),
            pltpu.VMEM((CHUNK, HEAD_DIM), BF16),
            pltpu.VMEM((kv_rows[1], HEAD_DIM), BF16),
            pltpu.VMEM((kv_rows[1], HEAD_DIM), BF16),
            pltpu.VMEM((CHUNK, HEAD_DIM), BF16),
            pltpu.VMEM((kv_rows[2], HEAD_DIM), BF16),
            pltpu.VMEM((kv_rows[2], HEAD_DIM), BF16),
            pltpu.VMEM((CHUNK, HEAD_DIM), F32),
            pltpu.VMEM((3, CHUNK, HEAD_DIM), F32),
            pltpu.VMEM((3, CHUNK, HEAD_DIM), F32),
        ],
        compiler_params=pltpu.CompilerParams(
            dimension_semantics=("arbitrary", "arbitrary", "arbitrary"),
            vmem_limit_bytes=VMEM_LIMIT_V7X),
        name="l0_attention",
    )(xn, rope, *([w_in] * 10), bias)


OUT_RES = 8


def _outproj_kernel(y_ref, w_ref, x_ref, g_ref, o_ref):
    p = jnp.dot(y_ref[...], w_ref[...], preferred_element_type=F32)
    g = g_ref[...]
    for i in range(OUT_RES):
        o_ref[:, i, :] = x_ref[:, i, :] + _rms(p[i * MM:(i + 1) * MM], g)


def _outproj(y, w_out, x3, g):
    b, seq, _ = y.shape
    n_chunks = seq // CHUNK
    steps = RES // OUT_RES
    return pl.pallas_call(
        _outproj_kernel,
        grid=(b, n_chunks, steps),
        in_specs=[
            pl.BlockSpec((None, OUT_RES * MM, E_WIDTH), lambda i, c, j: (i, c * steps + j, 0)),
            pl.BlockSpec((E_WIDTH, D_MODEL), lambda i, c, j: (0, 0)),
            pl.BlockSpec((None, MM, OUT_RES, D_MODEL), lambda i, c, j: (i, c, j, 0)),
            pl.BlockSpec((1, D_MODEL), lambda i, c, j: (0, 0)),
        ],
        out_specs=pl.BlockSpec((None, MM, OUT_RES, D_MODEL), lambda i, c, j: (i, c, j, 0)),
        out_shape=jax.ShapeDtypeStruct(x3.shape, F32),
        compiler_params=pltpu.CompilerParams(
            dimension_semantics=("arbitrary", "arbitrary", "arbitrary"),
            vmem_limit_bytes=VMEM_LIMIT_V7X),
        name="l0_outproj",
    )(y, w_out, x3, g)


POOL_TM = 512
POOL_HALO = 16


def _pool_kernel(h_ref, gpre_ref, win_ref, wgrp_ref, bgrp_ref, scale_ref, wout_ref, gpost_ref,
                 o_ref, carry_s, y_s):
    t = pl.program_id(1)

    @pl.when(t == 0)
    def _():
        carry_s[...] = jnp.zeros(carry_s.shape, carry_s.dtype)

    h = h_ref[...]
    xn = _rms(h, gpre_ref[...]).astype(BF16)
    pos = lax.broadcasted_iota(jnp.int32, (POOL_TM, 1), 0) + t * POOL_TM
    for g, w in enumerate(POOL_WINDOWS):
        cols = slice(g * POOL_CH, (g + 1) * POOL_CH)
        u = jnp.dot(xn, win_ref[:, cols], preferred_element_type=F32)
        z = jnp.dot(xn, win_ref[:, E_WIDTH + g * POOL_CH:E_WIDTH + (g + 1) * POOL_CH],
                    preferred_element_type=F32)
        s = jnp.concatenate([carry_s[:, cols], u], axis=0)
        k = 1
        while k < w:
            s = s + pltpu.roll(s, k, 0)
            k *= 2
        carry_s[:, cols] = u[POOL_TM - POOL_HALO:]
        inv_cnt = 1.0 / jnp.minimum(pos + 1, w).astype(F32)
        pooled = s[POOL_HALO:] * inv_cnt - u
        hg = jnp.dot(pooled.astype(BF16), wgrp_ref[g], preferred_element_type=F32) + bgrp_ref[:, cols]
        y_s[:, cols] = (hg * scale_ref[:, cols] * _silu(z)).astype(BF16)
    p = jnp.dot(y_s[...], wout_ref[...], preferred_element_type=F32)
    o_ref[...] = h + _rms(p, gpost_ref[...])


def _pool_layer(h, gpre, w_in, w_grp, b_grp, scale, w_out, gpost):
    b, seq, _ = h.shape
    const2 = lambda i, t: (0, 0)
    return pl.pallas_call(
        _pool_kernel,
        grid=(b, seq // POOL_TM),
        in_specs=[
            pl.BlockSpec((None, POOL_TM, D_MODEL), lambda i, t: (i, t, 0)),
            pl.BlockSpec((1, D_MODEL), const2),
            pl.BlockSpec((D_MODEL, 2 * E_WIDTH), const2),
            pl.BlockSpec((len(POOL_WINDOWS), POOL_CH, POOL_CH), lambda i, t: (0, 0, 0)),
            pl.BlockSpec((1, E_WIDTH), const2),
            pl.BlockSpec((1, E_WIDTH), const2),
            pl.BlockSpec((E_WIDTH, D_MODEL), const2),
            pl.BlockSpec((1, D_MODEL), const2),
        ],
        out_specs=pl.BlockSpec((None, POOL_TM, D_MODEL), lambda i, t: (i, t, 0)),
        out_shape=jax.ShapeDtypeStruct(h.shape, F32),
        scratch_shapes=[
            pltpu.VMEM((POOL_HALO, E_WIDTH), F32),
            pltpu.VMEM((POOL_TM, E_WIDTH), BF16),
        ],
        compiler_params=pltpu.CompilerParams(
            dimension_semantics=("arbitrary", "arbitrary"),
            vmem_limit_bytes=VMEM_LIMIT_V7X),
        name="l1_pool",
    )(h, gpre, w_in, w_grp, b_grp, scale, w_out, gpost)


def kernel(x, positions, norm_pre, norm_post, attn_w_in, attn_w_out,
           pool_w_in, pool_w_grp, pool_b_grp, pool_scale, pool_w_out):
    b, seq, d = x.shape
    assert d == D_MODEL and seq % CHUNK == 0
    assert attn_w_in.shape[0] == 1 and pool_w_in.shape[0] == 1
    inv_freq = ROPE_THETA ** (-jnp.arange(0, ROT_DIM, 2, dtype=F32) / ROT_DIM)
    invf = jnp.concatenate([inv_freq, inv_freq, jnp.zeros((HEAD_DIM - ROT_DIM,), F32)])[None, :]

    x3 = x.reshape(b, seq // RES, RES, D_MODEL)
    pos3 = positions.reshape(b, seq // RES, RES)
    xn, rope = _prenorm(x.reshape(b, seq // RES, RES * D_MODEL), pos3, norm_pre[0:1], invf)
    y = _attention(xn, rope, attn_w_in[0], _band_bias())
    h1 = _outproj(y, attn_w_out[0].astype(BF16), x3, norm_post[0:1]).reshape(b, seq, d)

    return _pool_layer(
        h1, norm_pre[1:2], pool_w_in[0].astype(BF16), pool_w_grp[0].astype(BF16),
        pool_b_grp[0].reshape(1, E_WIDTH), pool_scale[0:1], pool_w_out[0].astype(BF16), norm_post[1:2])
```

```python
import math

import numpy as np
import jax
import jax.numpy as jnp
from jax import lax
from jax.experimental import pallas as pl
from jax.experimental.pallas import tpu as pltpu

F32 = jnp.float32
BF16 = jnp.bfloat16

D_MODEL = 1024
E_WIDTH = 2048
HEAD_DIM = 128
N_HEADS = 16
ROT_DIM = 32
ROPE_THETA = 500000.0
RMS_EPS = 1e-6
NEG_INF = -1e30
POOL_WINDOWS = (2, 4, 8, 16)
POOL_CH = 512

RES = 16
MM = 128
CHUNK = RES * MM
QBLK = 128
HALO = (8, 32, 128)
PITCH = tuple(h + MM for h in HALO)
SCALE2 = math.log2(math.e) / math.sqrt(HEAD_DIM)
ATTN_UNROLL = 8
COMB_ROWS = 256

VMEM_LIMIT_V7X = 60000 * 1024


def _rms(x, g):
    ms = jnp.mean(x * x, axis=-1, keepdims=True)
    return x * lax.rsqrt(ms + RMS_EPS) * g


def _silu(z):
    hz = 0.5 * z
    return hz + hz * jnp.tanh(hz)


def _prenorm_kernel(x_ref, pos_ref, g_ref, invf_ref, xn_ref, rope_ref):
    g = g_ref[...]
    invf = invf_ref[...]
    lane = lax.broadcasted_iota(jnp.int32, (1, HEAD_DIM), 1)
    lo = (lane < ROT_DIM // 2).astype(F32)
    hi = ((lane >= ROT_DIM // 2) & (lane < ROT_DIM)).astype(F32)
    for r in range(RES):
        x = x_ref[:, r * D_MODEL:(r + 1) * D_MODEL]
        xn_ref[r * MM:(r + 1) * MM, :] = _rms(x, g).astype(BF16)
        ang = pos_ref[:, r:r + 1].astype(F32) * invf
        sn = jnp.sin(ang)
        rope_ref[r * MM:(r + 1) * MM, 0:128] = jnp.cos(ang)
        rope_ref[r * MM:(r + 1) * MM, 128:256] = -sn * lo
        rope_ref[r * MM:(r + 1) * MM, 256:384] = sn * hi


def _prenorm(x3, pos3, g, invf):
    b = x3.shape[0]
    n_chunks = x3.shape[1] // MM
    seq = n_chunks * CHUNK
    return pl.pallas_call(
        _prenorm_kernel,
        grid=(b, n_chunks),
        in_specs=[
            pl.BlockSpec((None, MM, RES * D_MODEL), lambda i, c: (i, c, 0)),
            pl.BlockSpec((None, MM, RES), lambda i, c: (i, c, 0)),
            pl.BlockSpec((1, D_MODEL), lambda i, c: (0, 0)),
            pl.BlockSpec((1, HEAD_DIM), lambda i, c: (0, 0)),
        ],
        out_specs=[
            pl.BlockSpec((None, CHUNK, D_MODEL), lambda i, c: (i, c, 0)),
            pl.BlockSpec((None, CHUNK, 3 * HEAD_DIM), lambda i, c: (i, c, 0)),
        ],
        out_shape=[
            jax.ShapeDtypeStruct((b, seq, D_MODEL), BF16),
            jax.ShapeDtypeStruct((b, seq, 3 * HEAD_DIM), F32),
        ],
        compiler_params=pltpu.CompilerParams(
            dimension_semantics=("arbitrary", "arbitrary"),
            vmem_limit_bytes=VMEM_LIMIT_V7X),
        name="l0_prenorm",
    )(x3, pos3, g, invf)


def _band_bias():
    qi = np.arange(QBLK)[:, None]
    kj = np.arange(2 * QBLK)[None, :]
    out = np.zeros((3, 2, QBLK, 2 * QBLK), np.float32)
    dist0 = 16 * ((qi % 8) - (kj % 16) + 8) + (qi // 8) - (kj // 16)
    halo0 = (kj % 16) < 8
    dist1 = 4 * ((qi % 32) - (kj % 64) + 32) + (qi // 32) - (kj // 64)
    halo1 = (kj % 64) < 32
    dist2 = qi - kj + 128
    halo2 = kj < 128
    for g, (dist, halo) in enumerate(((dist0, halo0), (dist1, halo1), (dist2, halo2))):
        band = (dist >= 0) & (dist <= QBLK)
        out[g, 0] = np.where(band, 0.0, NEG_INF)
        out[g, 1] = np.where(band & ~halo, 0.0, NEG_INF)
    return jnp.asarray(out)


def _attn_block(q, k, v, bias):
    s = lax.dot_general(q, k, (((1,), (1,)), ((), ())), preferred_element_type=F32) + bias
    m = jnp.max(s, axis=-1, keepdims=True)
    p = jnp.exp2(s - m).astype(BF16)
    va = jnp.concatenate([v, jnp.ones_like(v)], axis=1)
    od = jnp.dot(p, va, preferred_element_type=F32)
    return od[:, :HEAD_DIM], od[:, HEAD_DIM:], jnp.broadcast_to(m, (QBLK, HEAD_DIM))


def _attn_kernel(xn_ref, rope_ref, wq0, wk0, wv0, wz, wq1, wk1, wq2, wk2, wv1, wv2, bias_ref,
                 y_ref,
                 w_s, q0_s, k0_s, v0_s, q1_s, k1_s, v1_s, q2_s, k2_s, v2_s, z_s, acc_s, den_s, max_s):
    c = pl.program_id(2)
    kv_bufs = ((k0_s, v0_s), (k1_s, v1_s), (k2_s, v2_s))

    @pl.when(c == 0)
    def _():
        for j, w in enumerate((wq0, wk0, wv0, wz, wq1, wk1, wq2, wk2, wv1, wv2)):
            w_s[:, j * HEAD_DIM:(j + 1) * HEAD_DIM] = w[...].astype(BF16)
        for kb, vb in kv_bufs:
            kb[...] = jnp.zeros(kb.shape, kb.dtype)
            vb[...] = jnp.zeros(vb.shape, vb.dtype)

    x = xn_ref[...]
    cos = rope_ref[:, 0:128]
    sin_a = rope_ref[:, 128:256]
    sin_b = rope_ref[:, 256:384]

    def rope(t):
        return t * cos + pltpu.roll(t, HEAD_DIM - ROT_DIM // 2, 1) * sin_a + pltpu.roll(t, ROT_DIM // 2, 1) * sin_b

    def proj(tile):
        return jnp.dot(x, w_s[:, tile * 256:(tile + 1) * 256], preferred_element_type=F32)

    def put_cur(buf, g, val):
        for r in range(RES):
            buf[r * PITCH[g] + HALO[g]:(r + 1) * PITCH[g], :] = val[r * MM:(r + 1) * MM].astype(buf.dtype)

    t = proj(0)
    q0_s[...] = rope(t[:, :128]) * SCALE2
    put_cur(k0_s, 0, rope(t[:, 128:]))
    t = proj(1)
    put_cur(v0_s, 0, t[:, :128])
    z_s[...] = t[:, 128:]
    t = proj(2)
    q1_s[...] = (rope(t[:, :128]) * SCALE2).astype(BF16)
    put_cur(k1_s, 1, rope(t[:, 128:]))
    t = proj(3)
    q2_s[...] = (rope(t[:, :128]) * SCALE2).astype(BF16)
    put_cur(k2_s, 2, rope(t[:, 128:]))
    t = proj(4)
    put_cur(v1_s, 1, t[:, :128])
    put_cur(v2_s, 2, t[:, 128:])

    first = c == 0

    def body0(qb):
        off = qb * 8
        q = jnp.concatenate(
            [q0_s[pl.ds(pl.multiple_of(r * MM + off, 8), 8), :] for r in range(RES)], axis=0).astype(BF16)
        k = jnp.concatenate(
            [k0_s[pl.ds(pl.multiple_of(r * PITCH[0] + off, 8), 16), :] for r in range(RES)], axis=0).astype(BF16)
        v = jnp.concatenate(
            [v0_s[pl.ds(pl.multiple_of(r * PITCH[0] + off, 8), 16), :] for r in range(RES)], axis=0).astype(BF16)
        flag = jnp.logical_and(first, qb == 0).astype(jnp.int32)
        acc, den, mx = _attn_block(q, k, v, bias_ref[0, flag])
        for r in range(RES):
            rows = pl.ds(pl.multiple_of(r * MM + off, 8), 8)
            acc_s[0, rows, :] = acc[r * 8:(r + 1) * 8]
            den_s[0, rows, :] = den[r * 8:(r + 1) * 8]
            max_s[0, rows, :] = mx[r * 8:(r + 1) * 8]

    def body1(it):
        r4 = it // 4
        off = (it % 4) * 32
        q = jnp.concatenate(
            [q1_s[pl.ds(pl.multiple_of((r4 + 4 * cc) * MM + off, 32), 32), :] for cc in range(4)], axis=0)
        k = jnp.concatenate(
            [k1_s[pl.ds(pl.multiple_of((r4 + 4 * cc) * PITCH[1] + off, 32), 64), :] for cc in range(4)], axis=0)
        v = jnp.concatenate(
            [v1_s[pl.ds(pl.multiple_of((r4 + 4 * cc) * PITCH[1] + off, 32), 64), :] for cc in range(4)], axis=0)
        flag = jnp.logical_and(first, (it % 4) == 0).astype(jnp.int32)
        acc, den, mx = _attn_block(q, k, v, bias_ref[1, flag])
        for cc in range(4):
            rows = pl.ds(pl.multiple_of((r4 + 4 * cc) * MM + off, 32), 32)
            acc_s[1, rows, :] = acc[cc * 32:(cc + 1) * 32]
            den_s[1, rows, :] = den[cc * 32:(cc + 1) * 32]
            max_s[1, rows, :] = mx[cc * 32:(cc + 1) * 32]

    def body2(r):
        qrows = pl.ds(pl.multiple_of(r * MM, MM), MM)
        krows = pl.ds(pl.multiple_of(r * PITCH[2], PITCH[2]), PITCH[2])
        flag = first.astype(jnp.int32)
        acc, den, mx = _attn_block(q2_s[qrows, :], k2_s[krows, :], v2_s[krows, :], bias_ref[2, flag])
        acc_s[2, qrows, :] = acc
        den_s[2, qrows, :] = den
        max_s[2, qrows, :] = mx

    def blocks(i, carry):
        body0(i)
        body1(i)
        body2(i)
        return carry

    lax.fori_loop(0, RES, blocks, 0, unroll=ATTN_UNROLL)

    for g, (kb, vb) in enumerate(kv_bufs):
        for r in range(RES):
            base = r * PITCH[g]
            kb[base:base + HALO[g], :] = kb[base + MM:base + MM + HALO[g], :]
            vb[base:base + HALO[g], :] = vb[base + MM:base + MM + HALO[g], :]

    def comb(i, carry):
        rows = pl.ds(pl.multiple_of(i * COMB_ROWS, COMB_ROWS), COMB_ROWS)
        m0 = max_s[0, rows, :]
        m1 = max_s[1, rows, :]
        m2 = max_s[2, rows, :]
        mmax = jnp.maximum(jnp.maximum(m0, m1), m2)
        e0 = jnp.exp2(m0 - mmax)
        e1 = jnp.exp2(m1 - mmax)
        e2 = jnp.exp2(m2 - mmax)
        num = e0 * acc_s[0, rows, :] + e1 * acc_s[1, rows, :] + e2 * acc_s[2, rows, :]
        den = e0 * den_s[0, rows, :] + e1 * den_s[1, rows, :] + e2 * den_s[2, rows, :]
        y_ref[rows, :] = (num * (1.0 / den) * _silu(z_s[rows, :])).astype(BF16)
        return carry

    lax.fori_loop(0, CHUNK // COMB_ROWS, comb, 0)


def _attention(xn, rope, w_in, bias):
    b, seq, _ = xn.shape
    n_chunks = seq // CHUNK

    def wspec(col_block):
        return pl.BlockSpec((D_MODEL, HEAD_DIM), lambda i, h, c: (0, col_block + h))

    col = lambda g, j: (g * 3 + j) * N_HEADS
    w_specs = [wspec(col(0, 0)), wspec(col(0, 1)), wspec(col(0, 2)), wspec(9 * N_HEADS),
               wspec(col(1, 0)), wspec(col(1, 1)), wspec(col(2, 0)), wspec(col(2, 1)),
               wspec(col(1, 2)), wspec(col(2, 2))]
    kv_rows = [RES * p for p in PITCH]
    return pl.pallas_call(
        _attn_kernel,
        grid=(b, N_HEADS, n_chunks),
        in_specs=[
            pl.BlockSpec((None, CHUNK, D_MODEL), lambda i, h, c: (i, c, 0)),
            pl.BlockSpec((None, CHUNK, 3 * HEAD_DIM), lambda i, h, c: (i, c, 0)),
            *w_specs,
            pl.BlockSpec((3, 2, QBLK, 2 * QBLK), lambda i, h, c: (0, 0, 0, 0)),
        ],
        out_specs=pl.BlockSpec((None, CHUNK, HEAD_DIM), lambda i, h, c: (i, c, h)),
        out_shape=jax.ShapeDtypeStruct((b, seq, E_WIDTH), BF16),
        scratch_shapes=[
            pltpu.VMEM((D_MODEL, 10 * HEAD_DIM), BF16),
            pltpu.VMEM((CHUNK, HEAD_DIM), F32),
            pltpu.VMEM((kv_rows[0], HEAD_DIM), F32),
            pltpu.VMEM((kv_rows[0], HEAD_DIM), F32),
            pltpu.VMEM((CHUNK, HEAD_DIM), BF16),
            pltpu.VMEM((kv_rows[1], HEAD_DIM), BF16),
            pltpu.VMEM((kv_rows[1], HEAD_DIM), BF16),
            pltpu.VMEM((CHUNK, HEAD_DIM), BF16),
            pltpu.VMEM((kv_rows[2], HEAD_DIM), BF16),
            pltpu.VMEM((kv_rows[2], HEAD_DIM), BF16),
            pltpu.VMEM((CHUNK, HEAD_DIM), F32),
            pltpu.VMEM((3, CHUNK, HEAD_DIM), F32),
            pltpu.VMEM((3, CHUNK, HEAD_DIM), F32),
            pltpu.VMEM((3, CHUNK, HEAD_DIM), F32),
        ],
        compiler_params=pltpu.CompilerParams(
            dimension_semantics=("arbitrary", "arbitrary", "arbitrary"),
            vmem_limit_bytes=VMEM_LIMIT_V7X),
        name="l0_attention",
    )(xn, rope, *([w_in] * 10), bias)


OUT_RES = 8


def _outproj_kernel(y_ref, w_ref, x_ref, g_ref, o_ref):
    p = jnp.dot(y_ref[...], w_ref[...], preferred_element_type=F32)
    g = g_ref[...]
    for i in range(OUT_RES):
        o_ref[:, i, :] = x_ref[:, i, :] + _rms(p[i * MM:(i + 1) * MM], g)


def _outproj(y, w_out, x3, g):
    b, seq, _ = y.shape
    n_chunks = seq // CHUNK
    steps = RES // OUT_RES
    return pl.pallas_call(
        _outproj_kernel,
        grid=(b, n_chunks, steps),
        in_specs=[
            pl.BlockSpec((None, OUT_RES * MM, E_WIDTH), lambda i, c, j: (i, c * steps + j, 0)),
            pl.BlockSpec((E_WIDTH, D_MODEL), lambda i, c, j: (0, 0)),
            pl.BlockSpec((None, MM, OUT_RES, D_MODEL), lambda i, c, j: (i, c, j, 0)),
            pl.BlockSpec((1, D_MODEL), lambda i, c, j: (0, 0)),
        ],
        out_specs=pl.BlockSpec((None, MM, OUT_RES, D_MODEL), lambda i, c, j: (i, c, j, 0)),
        out_shape=jax.ShapeDtypeStruct(x3.shape, F32),
        compiler_params=pltpu.CompilerParams(
            dimension_semantics=("arbitrary", "arbitrary", "arbitrary"),
            vmem_limit_bytes=VMEM_LIMIT_V7X),
        name="l0_outproj",
    )(y, w_out, x3, g)


POOL_TM = 512
POOL_HALO = 16


def _pool_kernel(h_ref, gpre_ref, win_ref, wgrp_ref, bgrp_ref, scale_ref, wout_ref, gpost_ref,
                 o_ref, carry_s, y_s):
    t = pl.program_id(1)

    @pl.when(t == 0)
    def _():
        carry_s[...] = jnp.zeros(carry_s.shape, carry_s.dtype)

    h = h_ref[...]
    xn = _rms(h, gpre_ref[...]).astype(BF16)
    pos = lax.broadcasted_iota(jnp.int32, (POOL_TM, 1), 0) + t * POOL_TM
    for g, w in enumerate(POOL_WINDOWS):
        cols = slice(g * POOL_CH, (g + 1) * POOL_CH)
        u = jnp.dot(xn, win_ref[:, cols], preferred_element_type=F32)
        z = jnp.dot(xn, win_ref[:, E_WIDTH + g * POOL_CH:E_WIDTH + (g + 1) * POOL_CH],
                    preferred_element_type=F32)
        s = jnp.concatenate([carry_s[:, cols], u], axis=0)
        k = 1
        while k < w:
            s = s + pltpu.roll(s, k, 0)
            k *= 2
        carry_s[:, cols] = u[POOL_TM - POOL_HALO:]
        inv_cnt = 1.0 / jnp.minimum(pos + 1, w).astype(F32)
        pooled = s[POOL_HALO:] * inv_cnt - u
        hg = jnp.dot(pooled.astype(BF16), wgrp_ref[g], preferred_element_type=F32) + bgrp_ref[:, cols]
        y_s[:, cols] = (hg * scale_ref[:, cols] * _silu(z)).astype(BF16)
    p = jnp.dot(y_s[...], wout_ref[...], preferred_element_type=F32)
    o_ref[...] = h + _rms(p, gpost_ref[...])


def _pool_layer(h, gpre, w_in, w_grp, b_grp, scale, w_out, gpost):
    b, seq, _ = h.shape
    const2 = lambda i, t: (0, 0)
    return pl.pallas_call(
        _pool_kernel,
        grid=(b, seq // POOL_TM),
        in_specs=[
            pl.BlockSpec((None, POOL_TM, D_MODEL), lambda i, t: (i, t, 0)),
            pl.BlockSpec((1, D_MODEL), const2),
            pl.BlockSpec((D_MODEL, 2 * E_WIDTH), const2),
            pl.BlockSpec((len(POOL_WINDOWS), POOL_CH, POOL_CH), lambda i, t: (0, 0, 0)),
            pl.BlockSpec((1, E_WIDTH), const2),
            pl.BlockSpec((1, E_WIDTH), const2),
            pl.BlockSpec((E_WIDTH, D_MODEL), const2),
            pl.BlockSpec((1, D_MODEL), const2),
        ],
        out_specs=pl.BlockSpec((None, POOL_TM, D_MODEL), lambda i, t: (i, t, 0)),
        out_shape=jax.ShapeDtypeStruct(h.shape, F32),
        scratch_shapes=[
            pltpu.VMEM((POOL_HALO, E_WIDTH), F32),
            pltpu.VMEM((POOL_TM, E_WIDTH), BF16),
        ],
        compiler_params=pltpu.CompilerParams(
            dimension_semantics=("arbitrary", "arbitrary"),
            vmem_limit_bytes=VMEM_LIMIT_V7X),
        name="l1_pool",
    )(h, gpre, w_in, w_grp, b_grp, scale, w_out, gpost)


def kernel(x, positions, norm_pre, norm_post, attn_w_in, attn_w_out,
           pool_w_in, pool_w_grp, pool_b_grp, pool_scale, pool_w_out):
    b, seq, d = x.shape
    assert d == D_MODEL and seq % CHUNK == 0
    assert attn_w_in.shape[0] == 1 and pool_w_in.shape[0] == 1
    inv_freq = ROPE_THETA ** (-jnp.arange(0, ROT_DIM, 2, dtype=F32) / ROT_DIM)
    invf = jnp.concatenate([inv_freq, inv_freq, jnp.zeros((HEAD_DIM - ROT_DIM,), F32)])[None, :]

    x3 = x.reshape(b, seq // RES, RES, D_MODEL)
    pos3 = positions.reshape(b, seq // RES, RES)
    xn, rope = _prenorm(x.reshape(b, seq // RES, RES * D_MODEL), pos3, norm_pre[0:1], invf)
    y = _attention(xn, rope, attn_w_in[0], _band_bias())
    h1 = _outproj(y, attn_w_out[0].astype(BF16), x3, norm_post[0:1]).reshape(b, seq, d)

    return _pool_layer(
        h1, norm_pre[1:2], pool_w_in[0].astype(BF16), pool_w_grp[0].astype(BF16),
        pool_b_grp[0].reshape(1, E_WIDTH), pool_scale[0:1], pool_w_out[0].astype(BF16), norm_post[1:2])
```

```python
import math

import numpy as np
import jax
import jax.numpy as jnp
from jax import lax
from jax.experimental import pallas as pl
from jax.experimental.pallas import tpu as pltpu

F32 = jnp.float32
BF16 = jnp.bfloat16

D_MODEL = 1024
E_WIDTH = 2048
HEAD_DIM = 128
N_HEADS = 16
ROT_DIM = 32
ROPE_THETA = 500000.0
RMS_EPS = 1e-6
NEG_INF = -1e30
POOL_WINDOWS = (2, 4, 8, 16)
POOL_CH = 512

RES = 16
MM = 128
CHUNK = RES * MM
QBLK = 128
HALO = (8, 32, 128)
PITCH = tuple(h + MM for h in HALO)
SCALE2 = math.log2(math.e) / math.sqrt(HEAD_DIM)
ATTN_UNROLL = 8
COMB_ROWS = 256

VMEM_LIMIT_V7X = 60000 * 1024


def _rms(x, g):
    ms = jnp.mean(x * x, axis=-1, keepdims=True)
    return x * lax.rsqrt(ms + RMS_EPS) * g


def _silu(z):
    hz = 0.5 * z
    return hz + hz * jnp.tanh(hz)


LANES = 128
N_SLABS = D_MODEL // LANES


def _prenorm_kernel(*refs):
    x_slabs = refs[:N_SLABS]
    pos_ref, g_ref, invf_ref, xn_ref, rope_ref = refs[N_SLABS:]
    g = g_ref[...]
    half = ROT_DIM // 2
    wide = lax.broadcasted_iota(jnp.int32, (1, RES * half), 1)
    posf = pos_ref[...].astype(F32)
    pos_wide = jnp.zeros((MM, RES * half), F32)
    for r in range(RES):
        pos_wide = jnp.where(wide // half == r, posf[:, r:r + 1], pos_wide)
    ang = pos_wide * invf_ref[...]
    cos_all = jnp.cos(ang)
    sin_all = jnp.sin(ang)
    lane = lax.broadcasted_iota(jnp.int32, (1, LANES), 1)
    lo = lane < half
    hi = (lane >= half) & (lane < ROT_DIM)
    for r in range(RES):
        x = jnp.concatenate([s[pl.ds(r, MM, stride=RES), :] for s in x_slabs], axis=1)
        xn_ref[r * MM:(r + 1) * MM, :] = _rms(x, g).astype(BF16)
        tile = (r * half) // LANES
        off = (r * half) % LANES
        c_src = cos_all[:, tile * LANES:(tile + 1) * LANES]
        s_src = sin_all[:, tile * LANES:(tile + 1) * LANES]
        c_a = pltpu.roll(c_src, (LANES - off) % LANES, 1)
        c_b = pltpu.roll(c_src, (LANES - off + half) % LANES, 1)
        s_a = pltpu.roll(s_src, (LANES - off) % LANES, 1)
        s_b = pltpu.roll(s_src, (LANES - off + half) % LANES, 1)
        rows = slice(r * MM, (r + 1) * MM)
        rope_ref[rows, 0:128] = jnp.where(lo, c_a, jnp.where(hi, c_b, 1.0))
        rope_ref[rows, 128:256] = jnp.where(lo, -s_a, 0.0)
        rope_ref[rows, 256:384] = jnp.where(hi, s_b, 0.0)


def _prenorm(x, pos3, g, invf):
    b, seq, _ = x.shape
    n_chunks = seq // CHUNK
    return pl.pallas_call(
        _prenorm_kernel,
        grid=(b, n_chunks),
        in_specs=[
            *[pl.BlockSpec((None, CHUNK, LANES), lambda i, c, j=j: (i, c, j)) for j in range(N_SLABS)],
            pl.BlockSpec((None, MM, RES), lambda i, c: (i, c, 0)),
            pl.BlockSpec((1, D_MODEL), lambda i, c: (0, 0)),
            pl.BlockSpec((1, RES * ROT_DIM // 2), lambda i, c: (0, 0)),
        ],
        out_specs=[
            pl.BlockSpec((None, CHUNK, D_MODEL), lambda i, c: (i, c, 0)),
            pl.BlockSpec((None, CHUNK, 3 * HEAD_DIM), lambda i, c: (i, c, 0)),
        ],
        out_shape=[
            jax.ShapeDtypeStruct((b, seq, D_MODEL), BF16),
            jax.ShapeDtypeStruct((b, seq, 3 * HEAD_DIM), F32),
        ],
        compiler_params=pltpu.CompilerParams(
            dimension_semantics=("arbitrary", "arbitrary"),
            vmem_limit_bytes=VMEM_LIMIT_V7X),
        name="l0_prenorm",
    )(*([x] * N_SLABS), pos3, g, invf)


def _band_bias():
    qi = np.arange(QBLK)[:, None]
    kj = np.arange(2 * QBLK)[None, :]
    out = np.zeros((3, 2, QBLK, 2 * QBLK), np.float32)
    dist0 = 16 * ((qi % 8) - (kj % 16) + 8) + (qi // 8) - (kj // 16)
    halo0 = (kj % 16) < 8
    dist1 = 4 * ((qi % 32) - (kj % 64) + 32) + (qi // 32) - (kj // 64)
    halo1 = (kj % 64) < 32
    dist2 = qi - kj + 128
    halo2 = kj < 128
    for g, (dist, halo) in enumerate(((dist0, halo0), (dist1, halo1), (dist2, halo2))):
        band = (dist >= 0) & (dist <= QBLK)
        out[g, 0] = np.where(band, 0.0, NEG_INF)
        out[g, 1] = np.where(band & ~halo, 0.0, NEG_INF)
    return jnp.asarray(out)


def _attn_block(q, k, v, bias):
    s = lax.dot_general(q, k, (((1,), (1,)), ((), ())), preferred_element_type=F32) + bias
    m = jnp.max(s, axis=-1, keepdims=True)
    p = jnp.exp2(s - m).astype(BF16)
    va = jnp.concatenate([v, jnp.ones_like(v)], axis=1)
    od = jnp.dot(p, va, preferred_element_type=F32)
    return od[:, :HEAD_DIM], od[:, HEAD_DIM:], jnp.broadcast_to(m, (QBLK, HEAD_DIM))


def _attn_kernel(xn_ref, rope_ref, wq0, wk0, wv0, wz, wq1, wk1, wq2, wk2, wv1, wv2, bias_ref,
                 y_ref,
                 w_s, q0_s, k0_s, v0_s, q1_s, k1_s, v1_s, q2_s, k2_s, v2_s, z_s, acc_s, den_s, max_s):
    c = pl.program_id(2)
    kv_bufs = ((k0_s, v0_s), (k1_s, v1_s), (k2_s, v2_s))

    @pl.when(c == 0)
    def _():
        for j, w in enumerate((wq0, wk0, wv0, wz, wq1, wk1, wq2, wk2, wv1, wv2)):
            w_s[:, j * HEAD_DIM:(j + 1) * HEAD_DIM] = w[...].astype(BF16)
        for kb, vb in kv_bufs:
            kb[...] = jnp.zeros(kb.shape, kb.dtype)
            vb[...] = jnp.zeros(vb.shape, vb.dtype)

    x = xn_ref[...]
    cos = rope_ref[:, 0:128]
    sin_a = rope_ref[:, 128:256]
    sin_b = rope_ref[:, 256:384]

    def rope(t):
        return t * cos + pltpu.roll(t, HEAD_DIM - ROT_DIM // 2, 1) * sin_a + pltpu.roll(t, ROT_DIM // 2, 1) * sin_b

    def proj(tile):
        return jnp.dot(x, w_s[:, tile * 256:(tile + 1) * 256], preferred_element_type=F32)

    def put_cur(buf, g, val):
        for r in range(RES):
            buf[r * PITCH[g] + HALO[g]:(r + 1) * PITCH[g], :] = val[r * MM:(r + 1) * MM].astype(buf.dtype)

    t = proj(0)
    q0_s[...] = rope(t[:, :128]) * SCALE2
    put_cur(k0_s, 0, rope(t[:, 128:]))
    t = proj(1)
    put_cur(v0_s, 0, t[:, :128])
    z_s[...] = t[:, 128:]
    t = proj(2)
    q1_s[...] = (rope(t[:, :128]) * SCALE2).astype(BF16)
    put_cur(k1_s, 1, rope(t[:, 128:]))
    t = proj(3)
    q2_s[...] = (rope(t[:, :128]) * SCALE2).astype(BF16)
    put_cur(k2_s, 2, rope(t[:, 128:]))
    t = proj(4)
    put_cur(v1_s, 1, t[:, :128])
    put_cur(v2_s, 2, t[:, 128:])

    first = c == 0

    def body0(qb):
        off = qb * 8
        q = jnp.concatenate(
            [q0_s[pl.ds(pl.multiple_of(r * MM + off, 8), 8), :] for r in range(RES)], axis=0).astype(BF16)
        k = jnp.concatenate(
            [k0_s[pl.ds(pl.multiple_of(r * PITCH[0] + off, 8), 16), :] for r in range(RES)], axis=0).astype(BF16)
        v = jnp.concatenate(
            [v0_s[pl.ds(pl.multiple_of(r * PITCH[0] + off, 8), 16), :] for r in range(RES)], axis=0).astype(BF16)
        flag = jnp.logical_and(first, qb == 0).astype(jnp.int32)
        acc, den, mx = _attn_block(q, k, v, bias_ref[0, flag])
        for r in range(RES):
            rows = pl.ds(pl.multiple_of(r * MM + off, 8), 8)
            acc_s[0, rows, :] = acc[r * 8:(r + 1) * 8]
            den_s[0, rows, :] = den[r * 8:(r + 1) * 8]
            max_s[0, rows, :] = mx[r * 8:(r + 1) * 8]

    def body1(it):
        r4 = it // 4
        off = (it % 4) * 32
        q = jnp.concatenate(
            [q1_s[pl.ds(pl.multiple_of((r4 + 4 * cc) * MM + off, 32), 32), :] for cc in range(4)], axis=0)
        k = jnp.concatenate(
            [k1_s[pl.ds(pl.multiple_of((r4 + 4 * cc) * PITCH[1] + off, 32), 64), :] for cc in range(4)], axis=0)
        v = jnp.concatenate(
            [v1_s[pl.ds(pl.multiple_of((r4 + 4 * cc) * PITCH[1] + off, 32), 64), :] for cc in range(4)], axis=0)
        flag = jnp.logical_and(first, (it % 4) == 0).astype(jnp.int32)
        acc, den, mx = _attn_block(q, k, v, bias_ref[1, flag])
        for cc in range(4):
            rows = pl.ds(pl.multiple_of((r4 + 4 * cc) * MM + off, 32), 32)
            acc_s[1, rows, :] = acc[cc * 32:(cc + 1) * 32]
            den_s[1, rows, :] = den[cc * 32:(cc + 1) * 32]
            max_s[1, rows, :] = mx[cc * 32:(cc + 1) * 32]

    def body2(r):
        qrows = pl.ds(pl.multiple_of(r * MM, MM), MM)
        krows = pl.ds(pl.multiple_of(r * PITCH[2], PITCH[2]), PITCH[2])
        flag = first.astype(jnp.int32)
        acc, den, mx = _attn_block(q2_s[qrows, :], k2_s[krows, :], v2_s[krows, :], bias_ref[2, flag])
        acc_s[2, qrows, :] = acc
        den_s[2, qrows, :] = den
        max_s[2, qrows, :] = mx

    def blocks(i, carry):
        body0(i)
        body1(i)
        body2(i)
        return carry

    lax.fori_loop(0, RES, blocks, 0, unroll=ATTN_UNROLL)

    for g, (kb, vb) in enumerate(kv_bufs):
        for r in range(RES):
            base = r * PITCH[g]
            kb[base:base + HALO[g], :] = kb[base + MM:base + MM + HALO[g], :]
            vb[base:base + HALO[g], :] = vb[base + MM:base + MM + HALO[g], :]

    def comb(i, carry):
        rows = pl.ds(pl.multiple_of(i * COMB_ROWS, COMB_ROWS), COMB_ROWS)
        m0 = max_s[0, rows, :]
        m1 = max_s[1, rows, :]
        m2 = max_s[2, rows, :]
        mmax = jnp.maximum(jnp.maximum(m0, m1), m2)
        e0 = jnp.exp2(m0 - mmax)
        e1 = jnp.exp2(m1 - mmax)
        e2 = jnp.exp2(m2 - mmax)
        num = e0 * acc_s[0, rows, :] + e1 * acc_s[1, rows, :] + e2 * acc_s[2, rows, :]
        den = e0 * den_s[0, rows, :] + e1 * den_s[1, rows, :] + e2 * den_s[2, rows, :]
        y_ref[rows, :] = (num * (1.0 / den) * _silu(z_s[rows, :])).astype(BF16)
        return carry

    lax.fori_loop(0, CHUNK // COMB_ROWS, comb, 0)


def _attention(xn, rope, w_in, bias):
    b, seq, _ = xn.shape
    n_chunks = seq // CHUNK

    def wspec(col_block):
        return pl.BlockSpec((D_MODEL, HEAD_DIM), lambda i, h, c: (0, col_block + h))

    col = lambda g, j: (g * 3 + j) * N_HEADS
    w_specs = [wspec(col(0, 0)), wspec(col(0, 1)), wspec(col(0, 2)), wspec(9 * N_HEADS),
               wspec(col(1, 0)), wspec(col(1, 1)), wspec(col(2, 0)), wspec(col(2, 1)),
               wspec(col(1, 2)), wspec(col(2, 2))]
    kv_rows = [RES * p for p in PITCH]
    return pl.pallas_call(
        _attn_kernel,
        grid=(b, N_HEADS, n_chunks),
        in_specs=[
            pl.BlockSpec((None, CHUNK, D_MODEL), lambda i, h, c: (i, c, 0)),
            pl.BlockSpec((None, CHUNK, 3 * HEAD_DIM), lambda i, h, c: (i, c, 0)),
            *w_specs,
            pl.BlockSpec((3, 2, QBLK, 2 * QBLK), lambda i, h, c: (0, 0, 0, 0)),
        ],
        out_specs=pl.BlockSpec((None, CHUNK, HEAD_DIM), lambda i, h, c: (i, c, h)),
        out_shape=jax.ShapeDtypeStruct((b, seq, E_WIDTH), BF16),
        scratch_shapes=[
            pltpu.VMEM((D_MODEL, 10 * HEAD_DIM), BF16),
            pltpu.VMEM((CHUNK, HEAD_DIM), F32),
            pltpu.VMEM((kv_rows[0], HEAD_DIM), F32),
            pltpu.VMEM((kv_rows[0], HEAD_DIM), F32),
            pltpu.VMEM((CHUNK, HEAD_DIM), BF16),
            pltpu.VMEM((kv_rows[1], HEAD_DIM), BF16),
            pltpu.VMEM((kv_rows[1], HEAD_DIM), BF16),
            pltpu.VMEM((CHUNK, HEAD_DIM), BF16),
            pltpu.VMEM((kv_rows[2], HEAD_DIM), BF16),
            pltpu.VMEM((kv_rows[2], HEAD_DIM), BF16),
            pltpu.VMEM((CHUNK, HEAD_DIM), F32),
            pltpu.VMEM((3, CHUNK, HEAD_DIM), F32),
            pltpu.VMEM((3, CHUNK, HEAD_DIM), F32),
            pltpu.VMEM((3, CHUNK, HEAD_DIM), F32),
        ],
        compiler_params=pltpu.CompilerParams(
            dimension_semantics=("arbitrary", "arbitrary", "arbitrary"),
            vmem_limit_bytes=VMEM_LIMIT_V7X),
        name="l0_attention",
    )(xn, rope, *([w_in] * 10), bias)


OUT_RES = 8


def _outproj_kernel(y_ref, w_ref, x_ref, g_ref, o_ref):
    p = jnp.dot(y_ref[...], w_ref[...], preferred_element_type=F32)
    g = g_ref[...]
    for i in range(OUT_RES):
        o_ref[:, i, :] = x_ref[:, i, :] + _rms(p[i * MM:(i + 1) * MM], g)


def _outproj(y, w_out, x3, g):
    b, seq, _ = y.shape
    n_chunks = seq // CHUNK
    steps = RES // OUT_RES
    return pl.pallas_call(
        _outproj_kernel,
        grid=(b, n_chunks, steps),
        in_specs=[
            pl.BlockSpec((None, OUT_RES * MM, E_WIDTH), lambda i, c, j: (i, c * steps + j, 0)),
            pl.BlockSpec((E_WIDTH, D_MODEL), lambda i, c, j: (0, 0)),
            pl.BlockSpec((None, MM, OUT_RES, D_MODEL), lambda i, c, j: (i, c, j, 0)),
            pl.BlockSpec((1, D_MODEL), lambda i, c, j: (0, 0)),
        ],
        out_specs=pl.BlockSpec((None, MM, OUT_RES, D_MODEL), lambda i, c, j: (i, c, j, 0)),
        out_shape=jax.ShapeDtypeStruct(x3.shape, F32),
        compiler_params=pltpu.CompilerParams(
            dimension_semantics=("arbitrary", "arbitrary", "arbitrary"),
            vmem_limit_bytes=VMEM_LIMIT_V7X),
        name="l0_outproj",
    )(y, w_out, x3, g)


POOL_TM = 512
POOL_HALO = 16


def _pool_kernel(h_ref, gpre_ref, win_ref, wgrp_ref, bgrp_ref, scale_ref, wout_ref, gpost_ref,
                 o_ref, carry_s, y_s):
    t = pl.program_id(1)

    @pl.when(t == 0)
    def _():
        carry_s[...] = jnp.zeros(carry_s.shape, carry_s.dtype)

    h = h_ref[...]
    xn = _rms(h, gpre_ref[...]).astype(BF16)
    pos = lax.broadcasted_iota(jnp.int32, (POOL_TM, 1), 0) + t * POOL_TM
    for g, w in enumerate(POOL_WINDOWS):
        cols = slice(g * POOL_CH, (g + 1) * POOL_CH)
        u = jnp.dot(xn, win_ref[:, cols], preferred_element_type=F32)
        z = jnp.dot(xn, win_ref[:, E_WIDTH + g * POOL_CH:E_WIDTH + (g + 1) * POOL_CH],
                    preferred_element_type=F32)
        s = jnp.concatenate([carry_s[:, cols], u], axis=0)
        k = 1
        while k < w:
            s = s + pltpu.roll(s, k, 0)
            k *= 2
        carry_s[:, cols] = u[POOL_TM - POOL_HALO:]
        inv_cnt = 1.0 / jnp.minimum(pos + 1, w).astype(F32)
        pooled = s[POOL_HALO:] * inv_cnt - u
        hg = jnp.dot(pooled.astype(BF16), wgrp_ref[g], preferred_element_type=F32) + bgrp_ref[:, cols]
        y_s[:, cols] = (hg * scale_ref[:, cols] * _silu(z)).astype(BF16)
    p = jnp.dot(y_s[...], wout_ref[...], preferred_element_type=F32)
    o_ref[...] = h + _rms(p, gpost_ref[...])


def _pool_layer(h, gpre, w_in, w_grp, b_grp, scale, w_out, gpost):
    b, seq, _ = h.shape
    const2 = lambda i, t: (0, 0)
    return pl.pallas_call(
        _pool_kernel,
        grid=(b, seq // POOL_TM),
        in_specs=[
            pl.BlockSpec((None, POOL_TM, D_MODEL), lambda i, t: (i, t, 0)),
            pl.BlockSpec((1, D_MODEL), const2),
            pl.BlockSpec((D_MODEL, 2 * E_WIDTH), const2),
            pl.BlockSpec((len(POOL_WINDOWS), POOL_CH, POOL_CH), lambda i, t: (0, 0, 0)),
            pl.BlockSpec((1, E_WIDTH), const2),
            pl.BlockSpec((1, E_WIDTH), const2),
            pl.BlockSpec((E_WIDTH, D_MODEL), const2),
            pl.BlockSpec((1, D_MODEL), const2),
        ],
        out_specs=pl.BlockSpec((None, POOL_TM, D_MODEL), lambda i, t: (i, t, 0)),
        out_shape=jax.ShapeDtypeStruct(h.shape, F32),
        scratch_shapes=[
            pltpu.VMEM((POOL_HALO, E_WIDTH), F32),
            pltpu.VMEM((POOL_TM, E_WIDTH), BF16),
        ],
        compiler_params=pltpu.CompilerParams(
            dimension_semantics=("arbitrary", "arbitrary"),
            vmem_limit_bytes=VMEM_LIMIT_V7X),
        name="l1_pool",
    )(h, gpre, w_in, w_grp, b_grp, scale, w_out, gpost)


def kernel(x, positions, norm_pre, norm_post, attn_w_in, attn_w_out,
           pool_w_in, pool_w_grp, pool_b_grp, pool_scale, pool_w_out):
    b, seq, d = x.shape
    assert d == D_MODEL and seq % CHUNK == 0
    assert attn_w_in.shape[0] == 1 and pool_w_in.shape[0] == 1
    inv_freq = ROPE_THETA ** (-jnp.arange(0, ROT_DIM, 2, dtype=F32) / ROT_DIM)
    invf = jnp.tile(inv_freq, RES)[None, :]

    x3 = x.reshape(b, seq // RES, RES, D_MODEL)
    pos3 = positions.reshape(b, seq // RES, RES)
    xn, rope = _prenorm(x, pos3, norm_pre[0:1], invf)
    y = _attention(xn, rope, attn_w_in[0], _band_bias())
    h1 = _outproj(y, attn_w_out[0].astype(BF16), x3, norm_post[0:1]).reshape(b, seq, d)

    return _pool_layer(
        h1, norm_pre[1:2], pool_w_in[0].astype(BF16), pool_w_grp[0].astype(BF16),
        pool_b_grp[0].reshape(1, E_WIDTH), pool_scale[0:1], pool_w_out[0].astype(BF16), norm_post[1:2])
```

```python
import math

import numpy as np
import jax
import jax.numpy as jnp
from jax import lax
from jax.experimental import pallas as pl
from jax.experimental.pallas import tpu as pltpu

F32 = jnp.float32
BF16 = jnp.bfloat16

D_MODEL = 1024
E_WIDTH = 2048
HEAD_DIM = 128
N_HEADS = 16
ROT_DIM = 32
ROPE_THETA = 500000.0
RMS_EPS = 1e-6
NEG_INF = -1e30
POOL_WINDOWS = (2, 4, 8, 16)
POOL_CH = 512

RES = 16
MM = 128
CHUNK = RES * MM
QBLK = 128
HALO = (8, 32, 128)
PITCH = tuple(h + MM for h in HALO)
SCALE2 = math.log2(math.e) / math.sqrt(HEAD_DIM)
ATTN_UNROLL = 8
COMB_ROWS = 256

VMEM_LIMIT_V7X = 60000 * 1024


def _rms(x, g):
    ms = jnp.mean(x * x, axis=-1, keepdims=True)
    return x * lax.rsqrt(ms + RMS_EPS) * g


def _silu(z):
    hz = 0.5 * z
    return hz + hz * jnp.tanh(hz)


LANES = 128
N_SLABS = D_MODEL // LANES
ROPE_PARTNER = LANES // 2


def _rope_layout(w):
    half = ROT_DIM // 2
    rest = ROPE_PARTNER - half
    return jnp.concatenate(
        [w[:, :half], w[:, ROT_DIM:ROT_DIM + rest], w[:, half:ROT_DIM], w[:, ROT_DIM + rest:]], axis=1)


def _prenorm_kernel(*refs):
    x_slabs = refs[:N_SLABS]
    pos_ref, g_ref, invf_ref, xn_ref, rope_ref = refs[N_SLABS:]
    g = g_ref[...]
    half = ROT_DIM // 2
    wide = lax.broadcasted_iota(jnp.int32, (1, RES * half), 1)
    posf = pos_ref[...].astype(F32)
    pos_wide = jnp.zeros((MM, RES * half), F32)
    for r in range(RES):
        pos_wide = jnp.where(wide // half == r, posf[:, r:r + 1], pos_wide)
    ang = pos_wide * invf_ref[...]
    cos_all = jnp.cos(ang)
    sin_all = jnp.sin(ang)
    lane = lax.broadcasted_iota(jnp.int32, (1, LANES), 1)
    lo = lane < half
    hi = (lane >= ROPE_PARTNER) & (lane < ROPE_PARTNER + half)
    for r in range(RES):
        x = jnp.concatenate([s[pl.ds(r, MM, stride=RES), :] for s in x_slabs], axis=1)
        xn_ref[r * MM:(r + 1) * MM, :] = _rms(x, g).astype(BF16)
        tile = (r * half) // LANES
        off = (r * half) % LANES
        c_src = cos_all[:, tile * LANES:(tile + 1) * LANES]
        s_src = sin_all[:, tile * LANES:(tile + 1) * LANES]
        c_a = pltpu.roll(c_src, (LANES - off) % LANES, 1)
        c_b = pltpu.roll(c_src, (LANES - off + ROPE_PARTNER) % LANES, 1)
        s_a = pltpu.roll(s_src, (LANES - off) % LANES, 1)
        s_b = pltpu.roll(s_src, (LANES - off + ROPE_PARTNER) % LANES, 1)
        rows = slice(r * MM, (r + 1) * MM)
        rope_ref[rows, 0:LANES] = jnp.where(lo, c_a, jnp.where(hi, c_b, 1.0))
        rope_ref[rows, LANES:2 * LANES] = jnp.where(lo, -s_a, jnp.where(hi, s_b, 0.0))


def _prenorm(x, pos3, g, invf):
    b, seq, _ = x.shape
    n_chunks = seq // CHUNK
    return pl.pallas_call(
        _prenorm_kernel,
        grid=(b, n_chunks),
        in_specs=[
            *[pl.BlockSpec((None, CHUNK, LANES), lambda i, c, j=j: (i, c, j)) for j in range(N_SLABS)],
            pl.BlockSpec((None, MM, RES), lambda i, c: (i, c, 0)),
            pl.BlockSpec((1, D_MODEL), lambda i, c: (0, 0)),
            pl.BlockSpec((1, RES * ROT_DIM // 2), lambda i, c: (0, 0)),
        ],
        out_specs=[
            pl.BlockSpec((None, CHUNK, D_MODEL), lambda i, c: (i, c, 0)),
            pl.BlockSpec((None, CHUNK, 2 * LANES), lambda i, c: (i, c, 0)),
        ],
        out_shape=[
            jax.ShapeDtypeStruct((b, seq, D_MODEL), BF16),
            jax.ShapeDtypeStruct((b, seq, 2 * LANES), F32),
        ],
        compiler_params=pltpu.CompilerParams(
            dimension_semantics=("arbitrary", "arbitrary"),
            vmem_limit_bytes=VMEM_LIMIT_V7X),
        name="l0_prenorm",
    )(*([x] * N_SLABS), pos3, g, invf)


def _band_bias():
    qi = np.arange(QBLK)[:, None]
    kj = np.arange(2 * QBLK)[None, :]
    out = np.zeros((3, 2, QBLK, 2 * QBLK), np.float32)
    dist0 = 16 * ((qi % 8) - (kj % 16) + 8) + (qi // 8) - (kj // 16)
    halo0 = (kj % 16) < 8
    dist1 = 4 * ((qi % 32) - (kj % 64) + 32) + (qi // 32) - (kj // 64)
    halo1 = (kj % 64) < 32
    dist2 = qi - kj + 128
    halo2 = kj < 128
    for g, (dist, halo) in enumerate(((dist0, halo0), (dist1, halo1), (dist2, halo2))):
        band = (dist >= 0) & (dist <= QBLK)
        out[g, 0] = np.where(band, 0.0, NEG_INF)
        out[g, 1] = np.where(band & ~halo, 0.0, NEG_INF)
    return jnp.asarray(out)


def _attn_block(q, k, v, bias):
    s = lax.dot_general(q, k, (((1,), (1,)), ((), ())), preferred_element_type=F32) + bias
    m = jnp.max(s, axis=-1, keepdims=True)
    p = jnp.exp2(s - m).astype(BF16)
    va = jnp.concatenate([v, jnp.ones_like(v)], axis=1)
    od = jnp.dot(p, va, preferred_element_type=F32)
    return od[:, :HEAD_DIM], od[:, HEAD_DIM:], jnp.broadcast_to(m, (QBLK, HEAD_DIM))


def _attn_kernel(xn_ref, rope_ref, wq0, wk0, wv0, wz, wq1, wk1, wq2, wk2, wv1, wv2, bias_ref,
                 y_ref,
                 w_s, q0_s, k0_s, v0_s, q1_s, k1_s, v1_s, q2_s, k2_s, v2_s, z_s, acc_s, den_s, max_s):
    c = pl.program_id(2)
    kv_bufs = ((k0_s, v0_s), (k1_s, v1_s), (k2_s, v2_s))

    @pl.when(jnp.logical_and(pl.program_id(1) == 0, c == 0))
    def _():
        for j, w in enumerate((wq0, wk0, wv0, wz, wq1, wk1, wq2, wk2, wv1, wv2)):
            wj = _rope_layout(w[...]) if j in (0, 1, 4, 5, 6, 7) else w[...]
            w_s[:, j * HEAD_DIM:(j + 1) * HEAD_DIM] = wj.astype(BF16)

    @pl.when(c == 0)
    def _():
        for g, (kb, vb) in enumerate(kv_bufs):
            for r in range(RES):
                kb[r * PITCH[g]:r * PITCH[g] + HALO[g], :] = jnp.zeros((HALO[g], HEAD_DIM), kb.dtype)
                vb[r * PITCH[g]:r * PITCH[g] + HALO[g], :] = jnp.zeros((HALO[g], HEAD_DIM), vb.dtype)

    x = xn_ref[...]
    cos = rope_ref[:, 0:LANES]
    sin = rope_ref[:, LANES:2 * LANES]

    def rope(t):
        return t * cos + pltpu.roll(t, ROPE_PARTNER, 1) * sin

    def proj(tile):
        return jnp.dot(x, w_s[:, tile * 256:(tile + 1) * 256], preferred_element_type=F32)

    def put_cur(buf, g, val):
        for r in range(RES):
            buf[r * PITCH[g] + HALO[g]:(r + 1) * PITCH[g], :] = val[r * MM:(r + 1) * MM].astype(buf.dtype)

    t = proj(0)
    q0_s[...] = rope(t[:, :128]) * SCALE2
    put_cur(k0_s, 0, rope(t[:, 128:]))
    t = proj(1)
    put_cur(v0_s, 0, t[:, :128])
    z_s[...] = t[:, 128:]
    t = proj(2)
    q1_s[...] = (rope(t[:, :128]) * SCALE2).astype(BF16)
    put_cur(k1_s, 1, rope(t[:, 128:]))
    t = proj(3)
    q2_s[...] = (rope(t[:, :128]) * SCALE2).astype(BF16)
    put_cur(k2_s, 2, rope(t[:, 128:]))
    t = proj(4)
    put_cur(v1_s, 1, t[:, :128])
    put_cur(v2_s, 2, t[:, 128:])

    first = c == 0

    def body0(qb):
        off = qb * 8
        q = jnp.concatenate(
            [q0_s[pl.ds(pl.multiple_of(r * MM + off, 8), 8), :] for r in range(RES)], axis=0).astype(BF16)
        k = jnp.concatenate(
            [k0_s[pl.ds(pl.multiple_of(r * PITCH[0] + off, 8), 16), :] for r in range(RES)], axis=0).astype(BF16)
        v = jnp.concatenate(
            [v0_s[pl.ds(pl.multiple_of(r * PITCH[0] + off, 8), 16), :] for r in range(RES)], axis=0).astype(BF16)
        flag = jnp.logical_and(first, qb == 0).astype(jnp.int32)
        acc, den, mx = _attn_block(q, k, v, bias_ref[0, flag])
        for r in range(RES):
            rows = pl.ds(pl.multiple_of(r * MM + off, 8), 8)
            acc_s[0, rows, :] = acc[r * 8:(r + 1) * 8]
            den_s[0, rows, :] = den[r * 8:(r + 1) * 8]
            max_s[0, rows, :] = mx[r * 8:(r + 1) * 8]

    def body1(it):
        r4 = it // 4
        off = (it % 4) * 32
        q = jnp.concatenate(
            [q1_s[pl.ds(pl.multiple_of((r4 + 4 * cc) * MM + off, 32), 32), :] for cc in range(4)], axis=0)
        k = jnp.concatenate(
            [k1_s[pl.ds(pl.multiple_of((r4 + 4 * cc) * PITCH[1] + off, 32), 64), :] for cc in range(4)], axis=0)
        v = jnp.concatenate(
            [v1_s[pl.ds(pl.multiple_of((r4 + 4 * cc) * PITCH[1] + off, 32), 64), :] for cc in range(4)], axis=0)
        flag = jnp.logical_and(first, (it % 4) == 0).astype(jnp.int32)
        acc, den, mx = _attn_block(q, k, v, bias_ref[1, flag])
        for cc in range(4):
            rows = pl.ds(pl.multiple_of((r4 + 4 * cc) * MM + off, 32), 32)
            acc_s[1, rows, :] = acc[cc * 32:(cc + 1) * 32]
            den_s[1, rows, :] = den[cc * 32:(cc + 1) * 32]
            max_s[1, rows, :] = mx[cc * 32:(cc + 1) * 32]

    def body2(r):
        qrows = pl.ds(pl.multiple_of(r * MM, MM), MM)
        krows = pl.ds(pl.multiple_of(r * PITCH[2], PITCH[2]), PITCH[2])
        flag = first.astype(jnp.int32)
        acc, den, mx = _attn_block(q2_s[qrows, :], k2_s[krows, :], v2_s[krows, :], bias_ref[2, flag])
        acc_s[2, qrows, :] = acc
        den_s[2, qrows, :] = den
        max_s[2, qrows, :] = mx

    def blocks(i, carry):
        body0(i)
        body1(i)
        body2(i)
        return carry

    def all_blocks(_, carry):
        for i in range(RES):
            blocks(i, carry)
        return carry

    lax.fori_loop(0, jnp.minimum(c + 1, 1), all_blocks, 0)

    for g, (kb, vb) in enumerate(kv_bufs):
        for r in range(RES):
            base = r * PITCH[g]
            kb[base:base + HALO[g], :] = kb[base + MM:base + MM + HALO[g], :]
            vb[base:base + HALO[g], :] = vb[base + MM:base + MM + HALO[g], :]

    def comb(i, carry):
        rows = pl.ds(pl.multiple_of(i * COMB_ROWS, COMB_ROWS), COMB_ROWS)
        m0 = max_s[0, rows, :]
        m1 = max_s[1, rows, :]
        m2 = max_s[2, rows, :]
        mmax = jnp.maximum(jnp.maximum(m0, m1), m2)
        e0 = jnp.exp2(m0 - mmax)
        e1 = jnp.exp2(m1 - mmax)
        e2 = jnp.exp2(m2 - mmax)
        num = e0 * acc_s[0, rows, :] + e1 * acc_s[1, rows, :] + e2 * acc_s[2, rows, :]
        den = e0 * den_s[0, rows, :] + e1 * den_s[1, rows, :] + e2 * den_s[2, rows, :]
        y_ref[rows, :] = (num * (1.0 / den) * _silu(z_s[rows, :])).astype(BF16)
        return carry

    lax.fori_loop(0, CHUNK // COMB_ROWS, comb, 0)


def _attention(xn, rope, w_in, bias):
    b, seq, _ = xn.shape
    n_chunks = seq // CHUNK

    def wspec(col_block):
        return pl.BlockSpec((D_MODEL, HEAD_DIM), lambda h, i, c: (0, col_block + h))

    col = lambda g, j: (g * 3 + j) * N_HEADS
    w_specs = [wspec(col(0, 0)), wspec(col(0, 1)), wspec(col(0, 2)), wspec(9 * N_HEADS),
               wspec(col(1, 0)), wspec(col(1, 1)), wspec(col(2, 0)), wspec(col(2, 1)),
               wspec(col(1, 2)), wspec(col(2, 2))]
    kv_rows = [RES * p for p in PITCH]
    return pl.pallas_call(
        _attn_kernel,
        grid=(N_HEADS, b, n_chunks),
        in_specs=[
            pl.BlockSpec((None, CHUNK, D_MODEL), lambda h, i, c: (i, c, 0)),
            pl.BlockSpec((None, CHUNK, 2 * LANES), lambda h, i, c: (i, c, 0)),
            *w_specs,
            pl.BlockSpec((3, 2, QBLK, 2 * QBLK), lambda h, i, c: (0, 0, 0, 0)),
        ],
        out_specs=pl.BlockSpec((None, CHUNK, HEAD_DIM), lambda h, i, c: (i, c, h)),
        out_shape=jax.ShapeDtypeStruct((b, seq, E_WIDTH), BF16),
        scratch_shapes=[
            pltpu.VMEM((D_MODEL, 10 * HEAD_DIM), BF16),
            pltpu.VMEM((CHUNK, HEAD_DIM), F32),
            pltpu.VMEM((kv_rows[0], HEAD_DIM), F32),
            pltpu.VMEM((kv_rows[0], HEAD_DIM), F32),
            pltpu.VMEM((CHUNK, HEAD_DIM), BF16),
            pltpu.VMEM((kv_rows[1], HEAD_DIM), BF16),
            pltpu.VMEM((kv_rows[1], HEAD_DIM), BF16),
            pltpu.VMEM((CHUNK, HEAD_DIM), BF16),
            pltpu.VMEM((kv_rows[2], HEAD_DIM), BF16),
            pltpu.VMEM((kv_rows[2], HEAD_DIM), BF16),
            pltpu.VMEM((CHUNK, HEAD_DIM), F32),
            pltpu.VMEM((3, CHUNK, HEAD_DIM), F32),
            pltpu.VMEM((3, CHUNK, HEAD_DIM), F32),
            pltpu.VMEM((3, CHUNK, HEAD_DIM), F32),
        ],
        compiler_params=pltpu.CompilerParams(
            dimension_semantics=("arbitrary", "arbitrary", "arbitrary"),
            vmem_limit_bytes=VMEM_LIMIT_V7X),
        name="l0_attention",
    )(xn, rope, *([w_in] * 10), bias)


OUT_RES = 8


def _outproj_kernel(y_ref, w_ref, x_ref, g_ref, o_ref):
    p = jnp.dot(y_ref[...], w_ref[...], preferred_element_type=F32)
    g = g_ref[...]
    for i in range(OUT_RES):
        o_ref[:, i, :] = x_ref[:, i, :] + _rms(p[i * MM:(i + 1) * MM], g)


def _outproj(y, w_out, x3, g):
    b, seq, _ = y.shape
    n_chunks = seq // CHUNK
    steps = RES // OUT_RES
    return pl.pallas_call(
        _outproj_kernel,
        grid=(b, n_chunks, steps),
        in_specs=[
            pl.BlockSpec((None, OUT_RES * MM, E_WIDTH), lambda i, c, j: (i, c * steps + j, 0)),
            pl.BlockSpec((E_WIDTH, D_MODEL), lambda i, c, j: (0, 0)),
            pl.BlockSpec((None, MM, OUT_RES, D_MODEL), lambda i, c, j: (i, c, j, 0)),
            pl.BlockSpec((1, D_MODEL), lambda i, c, j: (0, 0)),
        ],
        out_specs=pl.BlockSpec((None, MM, OUT_RES, D_MODEL), lambda i, c, j: (i, c, j, 0)),
        out_shape=jax.ShapeDtypeStruct(x3.shape, F32),
        compiler_params=pltpu.CompilerParams(
            dimension_semantics=("arbitrary", "arbitrary", "arbitrary"),
            vmem_limit_bytes=VMEM_LIMIT_V7X),
        name="l0_outproj",
    )(y, w_out, x3, g)


POOL_TM = 512
POOL_HALO = 16


def _pool_kernel(h_ref, gpre_ref, win_ref, wgrp_ref, bgrp_ref, scale_ref, wout_ref, gpost_ref,
                 o_ref, carry_s, y_s):
    t = pl.program_id(1)

    @pl.when(t == 0)
    def _():
        carry_s[...] = jnp.zeros(carry_s.shape, carry_s.dtype)

    h = h_ref[...]
    xn = _rms(h, gpre_ref[...]).astype(BF16)
    pos = lax.broadcasted_iota(jnp.int32, (POOL_TM, 1), 0) + t * POOL_TM
    for g, w in enumerate(POOL_WINDOWS):
        cols = slice(g * POOL_CH, (g + 1) * POOL_CH)
        u = jnp.dot(xn, win_ref[:, cols], preferred_element_type=F32)
        z = jnp.dot(xn, win_ref[:, E_WIDTH + g * POOL_CH:E_WIDTH + (g + 1) * POOL_CH],
                    preferred_element_type=F32)
        s = jnp.concatenate([carry_s[:, cols], u], axis=0)
        k = 1
        while k < w:
            s = s + pltpu.roll(s, k, 0)
            k *= 2
        carry_s[:, cols] = u[POOL_TM - POOL_HALO:]
        inv_cnt = 1.0 / jnp.minimum(pos + 1, w).astype(F32)
        pooled = s[POOL_HALO:] * inv_cnt - u
        hg = jnp.dot(pooled.astype(BF16), wgrp_ref[g], preferred_element_type=F32) + bgrp_ref[:, cols]
        y_s[:, cols] = (hg * scale_ref[:, cols] * _silu(z)).astype(BF16)
    p = jnp.dot(y_s[...], wout_ref[...], preferred_element_type=F32)
    o_ref[...] = h + _rms(p, gpost_ref[...])


def _pool_layer(h, gpre, w_in, w_grp, b_grp, scale, w_out, gpost):
    b, seq, _ = h.shape
    const2 = lambda i, t: (0, 0)
    return pl.pallas_call(
        _pool_kernel,
        grid=(b, seq // POOL_TM),
        in_specs=[
            pl.BlockSpec((None, POOL_TM, D_MODEL), lambda i, t: (i, t, 0)),
            pl.BlockSpec((1, D_MODEL), const2),
            pl.BlockSpec((D_MODEL, 2 * E_WIDTH), const2),
            pl.BlockSpec((len(POOL_WINDOWS), POOL_CH, POOL_CH), lambda i, t: (0, 0, 0)),
            pl.BlockSpec((1, E_WIDTH), const2),
            pl.BlockSpec((1, E_WIDTH), const2),
            pl.BlockSpec((E_WIDTH, D_MODEL), const2),
            pl.BlockSpec((1, D_MODEL), const2),
        ],
        out_specs=pl.BlockSpec((None, POOL_TM, D_MODEL), lambda i, t: (i, t, 0)),
        out_shape=jax.ShapeDtypeStruct(h.shape, F32),
        scratch_shapes=[
            pltpu.VMEM((POOL_HALO, E_WIDTH), F32),
            pltpu.VMEM((POOL_TM, E_WIDTH), BF16),
        ],
        compiler_params=pltpu.CompilerParams(
            dimension_semantics=("arbitrary", "arbitrary"),
            vmem_limit_bytes=VMEM_LIMIT_V7X),
        name="l1_pool",
    )(h, gpre, w_in, w_grp, b_grp, scale, w_out, gpost)


def kernel(x, positions, norm_pre, norm_post, attn_w_in, attn_w_out,
           pool_w_in, pool_w_grp, pool_b_grp, pool_scale, pool_w_out):
    b, seq, d = x.shape
    assert d == D_MODEL and seq % CHUNK == 0
    assert attn_w_in.shape[0] == 1 and pool_w_in.shape[0] == 1
    inv_freq = ROPE_THETA ** (-jnp.arange(0, ROT_DIM, 2, dtype=F32) / ROT_DIM)
    invf = jnp.tile(inv_freq, RES)[None, :]

    x3 = x.reshape(b, seq // RES, RES, D_MODEL)
    pos3 = positions.reshape(b, seq // RES, RES)
    xn, rope = _prenorm(x, pos3, norm_pre[0:1], invf)
    y = _attention(xn, rope, attn_w_in[0], _band_bias())
    h1 = _outproj(y, attn_w_out[0].astype(BF16), x3, norm_post[0:1]).reshape(b, seq, d)

    return _pool_layer(
        h1, norm_pre[1:2], pool_w_in[0].astype(BF16), pool_w_grp[0].astype(BF16),
        pool_b_grp[0].reshape(1, E_WIDTH), pool_scale[0:1], pool_w_out[0].astype(BF16), norm_post[1:2])
```

```python
import math

import numpy as np
import jax
import jax.numpy as jnp
from jax import lax
from jax.experimental import pallas as pl
from jax.experimental.pallas import tpu as pltpu

F32 = jnp.float32
BF16 = jnp.bfloat16

D_MODEL = 1024
E_WIDTH = 2048
HEAD_DIM = 128
N_HEADS = 16
ROT_DIM = 32
ROPE_THETA = 500000.0
RMS_EPS = 1e-6
NEG_INF = -1e30
POOL_WINDOWS = (2, 4, 8, 16)
POOL_CH = 512

RES = 16
MM = 128
CHUNK = RES * MM
QBLK = 128
HALO = (8, 32, 128)
PITCH = tuple(h + MM for h in HALO)
SCALE2 = math.log2(math.e) / math.sqrt(HEAD_DIM)
ATTN_UNROLL = 8
COMB_ROWS = 256

VMEM_LIMIT_V7X = 60000 * 1024


def _rms(x, g):
    ms = jnp.mean(x * x, axis=-1, keepdims=True)
    return x * lax.rsqrt(ms + RMS_EPS) * g


def _silu(z):
    hz = 0.5 * z
    return hz + hz * jnp.tanh(hz)


LANES = 128
N_SLABS = D_MODEL // LANES
ROPE_PARTNER = LANES // 2


def _rope_layout(w):
    half = ROT_DIM // 2
    rest = ROPE_PARTNER - half
    return jnp.concatenate(
        [w[:, :half], w[:, ROT_DIM:ROT_DIM + rest], w[:, half:ROT_DIM], w[:, ROT_DIM + rest:]], axis=1)


def _prenorm_kernel(*refs):
    x_slabs = refs[:N_SLABS]
    pos_ref, g_ref, invf_ref, xn_ref, rope_ref = refs[N_SLABS:]
    g = g_ref[...]
    half = ROT_DIM // 2
    wide = lax.broadcasted_iota(jnp.int32, (1, RES * half), 1)
    posf = pos_ref[...].astype(F32)
    pos_wide = jnp.zeros((MM, RES * half), F32)
    for r in range(RES):
        pos_wide = jnp.where(wide // half == r, posf[:, r:r + 1], pos_wide)
    ang = pos_wide * invf_ref[...]
    cos_all = jnp.cos(ang)
    sin_all = jnp.sin(ang)
    lane = lax.broadcasted_iota(jnp.int32, (1, LANES), 1)
    lo = lane < half
    hi = (lane >= ROPE_PARTNER) & (lane < ROPE_PARTNER + half)
    for r in range(RES):
        x = jnp.concatenate([s[pl.ds(r, MM, stride=RES), :] for s in x_slabs], axis=1)
        xn_ref[r * MM:(r + 1) * MM, :] = _rms(x, g).astype(BF16)
        tile = (r * half) // LANES
        off = (r * half) % LANES
        c_src = cos_all[:, tile * LANES:(tile + 1) * LANES]
        s_src = sin_all[:, tile * LANES:(tile + 1) * LANES]
        c_a = pltpu.roll(c_src, (LANES - off) % LANES, 1)
        c_b = pltpu.roll(c_src, (LANES - off + ROPE_PARTNER) % LANES, 1)
        s_a = pltpu.roll(s_src, (LANES - off) % LANES, 1)
        s_b = pltpu.roll(s_src, (LANES - off + ROPE_PARTNER) % LANES, 1)
        rows = slice(r * MM, (r + 1) * MM)
        rope_ref[rows, 0:LANES] = jnp.where(lo, c_a, jnp.where(hi, c_b, 1.0))
        rope_ref[rows, LANES:2 * LANES] = jnp.where(lo, -s_a, jnp.where(hi, s_b, 0.0))


def _prenorm(x, pos3, g, invf):
    b, seq, _ = x.shape
    n_chunks = seq // CHUNK
    return pl.pallas_call(
        _prenorm_kernel,
        grid=(b, n_chunks),
        in_specs=[
            *[pl.BlockSpec((None, CHUNK, LANES), lambda i, c, j=j: (i, c, j)) for j in range(N_SLABS)],
            pl.BlockSpec((None, MM, RES), lambda i, c: (i, c, 0)),
            pl.BlockSpec((1, D_MODEL), lambda i, c: (0, 0)),
            pl.BlockSpec((1, RES * ROT_DIM // 2), lambda i, c: (0, 0)),
        ],
        out_specs=[
            pl.BlockSpec((None, CHUNK, D_MODEL), lambda i, c: (i, c, 0)),
            pl.BlockSpec((None, CHUNK, 2 * LANES), lambda i, c: (i, c, 0)),
        ],
        out_shape=[
            jax.ShapeDtypeStruct((b, seq, D_MODEL), BF16),
            jax.ShapeDtypeStruct((b, seq, 2 * LANES), F32),
        ],
        compiler_params=pltpu.CompilerParams(
            dimension_semantics=("arbitrary", "arbitrary"),
            vmem_limit_bytes=VMEM_LIMIT_V7X),
        name="l0_prenorm",
    )(*([x] * N_SLABS), pos3, g, invf)


def _band_bias():
    qi = np.arange(QBLK)[:, None]
    kj = np.arange(2 * QBLK)[None, :]
    out = np.zeros((3, 2, QBLK, 2 * QBLK), np.float32)
    dist0 = 16 * ((qi % 8) - (kj % 16) + 8) + (qi // 8) - (kj // 16)
    halo0 = (kj % 16) < 8
    dist1 = 4 * ((qi % 32) - (kj % 64) + 32) + (qi // 32) - (kj // 64)
    halo1 = (kj % 64) < 32
    dist2 = qi - kj + 128
    halo2 = kj < 128
    for g, (dist, halo) in enumerate(((dist0, halo0), (dist1, halo1), (dist2, halo2))):
        band = (dist >= 0) & (dist <= QBLK)
        out[g, 0] = np.where(band, 0.0, NEG_INF)
        out[g, 1] = np.where(band & ~halo, 0.0, NEG_INF)
    return jnp.asarray(out)


def _attn_block(q, k, v, bias):
    s = lax.dot_general(q, k, (((1,), (1,)), ((), ())), preferred_element_type=F32) + bias
    m = jnp.max(s, axis=-1, keepdims=True)
    p = jnp.exp2(s - m).astype(BF16)
    va = jnp.concatenate([v, jnp.ones_like(v)], axis=1)
    od = jnp.dot(p, va, preferred_element_type=F32)
    return od[:, :HEAD_DIM], od[:, HEAD_DIM:], jnp.broadcast_to(m, (QBLK, HEAD_DIM))


def _attn_kernel(xn_ref, rope_ref, wq0, wk0, wv0, wz, wq1, wk1, wq2, wk2, wv1, wv2, bias_ref,
                 y_ref,
                 w_s, q0_s, k0_s, v0_s, q1_s, k1_s, v1_s, q2_s, k2_s, v2_s, z_s, acc_s, den_s, max_s):
    c = pl.program_id(2)
    kv_bufs = ((k0_s, v0_s), (k1_s, v1_s), (k2_s, v2_s))

    @pl.when(jnp.logical_and(pl.program_id(1) == 0, c == 0))
    def _():
        for j, w in enumerate((wq0, wk0, wv0, wz, wq1, wk1, wq2, wk2, wv1, wv2)):
            wj = _rope_layout(w[...]) if j in (0, 1, 4, 5, 6, 7) else w[...]
            w_s[:, j * HEAD_DIM:(j + 1) * HEAD_DIM] = wj.astype(BF16)

    @pl.when(c == 0)
    def _():
        for g, (kb, vb) in enumerate(kv_bufs):
            for r in range(RES):
                kb[r * PITCH[g]:r * PITCH[g] + HALO[g], :] = jnp.zeros((HALO[g], HEAD_DIM), kb.dtype)
                vb[r * PITCH[g]:r * PITCH[g] + HALO[g], :] = jnp.zeros((HALO[g], HEAD_DIM), vb.dtype)

    x = xn_ref[...]
    cos = rope_ref[:, 0:LANES]
    sin = rope_ref[:, LANES:2 * LANES]

    def rope(t):
        return t * cos + pltpu.roll(t, ROPE_PARTNER, 1) * sin

    def proj(tile):
        return jnp.dot(x, w_s[:, tile * 256:(tile + 1) * 256], preferred_element_type=F32)

    def put_cur(buf, g, val):
        for r in range(RES):
            buf[r * PITCH[g] + HALO[g]:(r + 1) * PITCH[g], :] = val[r * MM:(r + 1) * MM].astype(buf.dtype)

    t = proj(0)
    q0_s[...] = rope(t[:, :128]) * SCALE2
    put_cur(k0_s, 0, rope(t[:, 128:]))
    t = proj(1)
    put_cur(v0_s, 0, t[:, :128])
    z_s[...] = _silu(t[:, 128:])
    t = proj(2)
    q1_s[...] = (rope(t[:, :128]) * SCALE2).astype(BF16)
    put_cur(k1_s, 1, rope(t[:, 128:]))
    t = proj(3)
    q2_s[...] = (rope(t[:, :128]) * SCALE2).astype(BF16)
    put_cur(k2_s, 2, rope(t[:, 128:]))
    t = proj(4)
    put_cur(v1_s, 1, t[:, :128])
    put_cur(v2_s, 2, t[:, 128:])

    first = c == 0

    def body0(qb):
        off = qb * 8
        q = jnp.concatenate(
            [q0_s[pl.ds(pl.multiple_of(r * MM + off, 8), 8), :] for r in range(RES)], axis=0).astype(BF16)
        k = jnp.concatenate(
            [k0_s[pl.ds(pl.multiple_of(r * PITCH[0] + off, 8), 16), :] for r in range(RES)], axis=0).astype(BF16)
        v = jnp.concatenate(
            [v0_s[pl.ds(pl.multiple_of(r * PITCH[0] + off, 8), 16), :] for r in range(RES)], axis=0).astype(BF16)
        flag = jnp.logical_and(first, qb == 0).astype(jnp.int32)
        acc, den, mx = _attn_block(q, k, v, bias_ref[0, flag])
        for r in range(RES):
            rows = pl.ds(pl.multiple_of(r * MM + off, 8), 8)
            acc_s[0, rows, :] = acc[r * 8:(r + 1) * 8]
            den_s[0, rows, :] = den[r * 8:(r + 1) * 8]
            max_s[0, rows, :] = mx[r * 8:(r + 1) * 8]

    def body1(it):
        r4 = it // 4
        off = (it % 4) * 32
        q = jnp.concatenate(
            [q1_s[pl.ds(pl.multiple_of((r4 + 4 * cc) * MM + off, 32), 32), :] for cc in range(4)], axis=0)
        k = jnp.concatenate(
            [k1_s[pl.ds(pl.multiple_of((r4 + 4 * cc) * PITCH[1] + off, 32), 64), :] for cc in range(4)], axis=0)
        v = jnp.concatenate(
            [v1_s[pl.ds(pl.multiple_of((r4 + 4 * cc) * PITCH[1] + off, 32), 64), :] for cc in range(4)], axis=0)
        flag = jnp.logical_and(first, (it % 4) == 0).astype(jnp.int32)
        acc, den, mx = _attn_block(q, k, v, bias_ref[1, flag])
        for cc in range(4):
            rows = pl.ds(pl.multiple_of((r4 + 4 * cc) * MM + off, 32), 32)
            acc_s[1, rows, :] = acc[cc * 32:(cc + 1) * 32]
            den_s[1, rows, :] = den[cc * 32:(cc + 1) * 32]
            max_s[1, rows, :] = mx[cc * 32:(cc + 1) * 32]

    def body2(r):
        qrows = pl.ds(pl.multiple_of(r * MM, MM), MM)
        krows = pl.ds(pl.multiple_of(r * PITCH[2], PITCH[2]), PITCH[2])
        flag = first.astype(jnp.int32)
        acc, den, mx = _attn_block(q2_s[qrows, :], k2_s[krows, :], v2_s[krows, :], bias_ref[2, flag])
        acc_s[2, qrows, :] = acc
        den_s[2, qrows, :] = den
        max_s[2, qrows, :] = mx

    def blocks(i, carry):
        body0(i)
        body1(i)
        body2(i)
        return carry

    def all_blocks(_, carry):
        for i in range(RES):
            blocks(i, carry)
        return carry

    lax.fori_loop(0, jnp.minimum(c + 1, 1), all_blocks, 0)

    for g, (kb, vb) in enumerate(kv_bufs):
        for r in range(RES):
            base = r * PITCH[g]
            kb[base:base + HALO[g], :] = kb[base + MM:base + MM + HALO[g], :]
            vb[base:base + HALO[g], :] = vb[base + MM:base + MM + HALO[g], :]

    def comb(i, carry):
        rows = pl.ds(pl.multiple_of(i * COMB_ROWS, COMB_ROWS), COMB_ROWS)
        m0 = max_s[0, rows, :]
        m1 = max_s[1, rows, :]
        m2 = max_s[2, rows, :]
        mmax = jnp.maximum(jnp.maximum(m0, m1), m2)
        e0 = jnp.exp2(m0 - mmax)
        e1 = jnp.exp2(m1 - mmax)
        e2 = jnp.exp2(m2 - mmax)
        num = e0 * acc_s[0, rows, :] + e1 * acc_s[1, rows, :] + e2 * acc_s[2, rows, :]
        den = e0 * den_s[0, rows, :] + e1 * den_s[1, rows, :] + e2 * den_s[2, rows, :]
        y_ref[rows, :] = (num * (1.0 / den) * z_s[rows, :]).astype(BF16)
        return carry

    lax.fori_loop(0, CHUNK // COMB_ROWS, comb, 0)


def _attention(xn, rope, w_in, bias):
    b, seq, _ = xn.shape
    n_chunks = seq // CHUNK

    def wspec(col_block):
        return pl.BlockSpec((D_MODEL, HEAD_DIM), lambda h, i, c: (0, col_block + h))

    col = lambda g, j: (g * 3 + j) * N_HEADS
    w_specs = [wspec(col(0, 0)), wspec(col(0, 1)), wspec(col(0, 2)), wspec(9 * N_HEADS),
               wspec(col(1, 0)), wspec(col(1, 1)), wspec(col(2, 0)), wspec(col(2, 1)),
               wspec(col(1, 2)), wspec(col(2, 2))]
    kv_rows = [RES * p for p in PITCH]
    return pl.pallas_call(
        _attn_kernel,
        grid=(N_HEADS, b, n_chunks),
        in_specs=[
            pl.BlockSpec((None, CHUNK, D_MODEL), lambda h, i, c: (i, c, 0)),
            pl.BlockSpec((None, CHUNK, 2 * LANES), lambda h, i, c: (i, c, 0)),
            *w_specs,
            pl.BlockSpec((3, 2, QBLK, 2 * QBLK), lambda h, i, c: (0, 0, 0, 0)),
        ],
        out_specs=pl.BlockSpec((None, CHUNK, HEAD_DIM), lambda h, i, c: (i, c, h)),
        out_shape=jax.ShapeDtypeStruct((b, seq, E_WIDTH), BF16),
        scratch_shapes=[
            pltpu.VMEM((D_MODEL, 10 * HEAD_DIM), BF16),
            pltpu.VMEM((CHUNK, HEAD_DIM), F32),
            pltpu.VMEM((kv_rows[0], HEAD_DIM), F32),
            pltpu.VMEM((kv_rows[0], HEAD_DIM), F32),
            pltpu.VMEM((CHUNK, HEAD_DIM), BF16),
            pltpu.VMEM((kv_rows[1], HEAD_DIM), BF16),
            pltpu.VMEM((kv_rows[1], HEAD_DIM), BF16),
            pltpu.VMEM((CHUNK, HEAD_DIM), BF16),
            pltpu.VMEM((kv_rows[2], HEAD_DIM), BF16),
            pltpu.VMEM((kv_rows[2], HEAD_DIM), BF16),
            pltpu.VMEM((CHUNK, HEAD_DIM), F32),
            pltpu.VMEM((3, CHUNK, HEAD_DIM), F32),
            pltpu.VMEM((3, CHUNK, HEAD_DIM), F32),
            pltpu.VMEM((3, CHUNK, HEAD_DIM), F32),
        ],
        compiler_params=pltpu.CompilerParams(
            dimension_semantics=("arbitrary", "arbitrary", "arbitrary"),
            vmem_limit_bytes=VMEM_LIMIT_V7X),
        name="l0_attention",
    )(xn, rope, *([w_in] * 10), bias)


OUT_RES = 8


def _outproj_kernel(y_ref, w_ref, x_ref, g_ref, o_ref):
    p = jnp.dot(y_ref[...], w_ref[...], preferred_element_type=F32)
    n = _rms(p, g_ref[...]).reshape(OUT_RES, MM, D_MODEL)
    o_ref[...] = x_ref[...] + jnp.swapaxes(n, 0, 1)


def _outproj(y, w_out, x3, g):
    b, seq, _ = y.shape
    n_chunks = seq // CHUNK
    steps = RES // OUT_RES
    return pl.pallas_call(
        _outproj_kernel,
        grid=(b, n_chunks, steps),
        in_specs=[
            pl.BlockSpec((None, OUT_RES * MM, E_WIDTH), lambda i, c, j: (i, c * steps + j, 0)),
            pl.BlockSpec((E_WIDTH, D_MODEL), lambda i, c, j: (0, 0)),
            pl.BlockSpec((None, MM, OUT_RES, D_MODEL), lambda i, c, j: (i, c, j, 0)),
            pl.BlockSpec((1, D_MODEL), lambda i, c, j: (0, 0)),
        ],
        out_specs=pl.BlockSpec((None, MM, OUT_RES, D_MODEL), lambda i, c, j: (i, c, j, 0)),
        out_shape=jax.ShapeDtypeStruct(x3.shape, F32),
        compiler_params=pltpu.CompilerParams(
            dimension_semantics=("arbitrary", "arbitrary", "arbitrary"),
            vmem_limit_bytes=VMEM_LIMIT_V7X),
        name="l0_outproj",
    )(y, w_out, x3, g)


POOL_TM = 1024
POOL_HALO = 16


def _pool_kernel(h_ref, gpre_ref, win_ref, wgrp_ref, bgrp_ref, scale_ref, wout_ref, gpost_ref,
                 o_ref, carry_s, y_s):
    t = pl.program_id(1)

    @pl.when(t == 0)
    def _():
        carry_s[...] = jnp.zeros(carry_s.shape, carry_s.dtype)

    h = h_ref[...]
    xn = _rms(h, gpre_ref[...]).astype(BF16)
    pos = lax.broadcasted_iota(jnp.int32, (POOL_TM, 1), 0) + t * POOL_TM
    for g, w in enumerate(POOL_WINDOWS):
        cols = slice(g * POOL_CH, (g + 1) * POOL_CH)
        u = jnp.dot(xn, win_ref[:, cols], preferred_element_type=F32)
        z = jnp.dot(xn, win_ref[:, E_WIDTH + g * POOL_CH:E_WIDTH + (g + 1) * POOL_CH],
                    preferred_element_type=F32)
        s = jnp.concatenate([carry_s[:, cols], u], axis=0)
        k = 1
        while k < w:
            s = s + pltpu.roll(s, k, 0)
            k *= 2
        carry_s[:, cols] = u[POOL_TM - POOL_HALO:]
        inv_cnt = 1.0 / jnp.minimum(pos + 1, w).astype(F32)
        pooled = s[POOL_HALO:] * inv_cnt - u
        hg = jnp.dot(pooled.astype(BF16), wgrp_ref[g], preferred_element_type=F32) + bgrp_ref[:, cols]
        y_s[:, cols] = (hg * scale_ref[:, cols] * _silu(z)).astype(BF16)
    p = jnp.dot(y_s[...], wout_ref[...], preferred_element_type=F32)
    o_ref[...] = h + _rms(p, gpost_ref[...])


def _pool_layer(h, gpre, w_in, w_grp, b_grp, scale, w_out, gpost):
    b, seq, _ = h.shape
    const2 = lambda i, t: (0, 0)
    return pl.pallas_call(
        _pool_kernel,
        grid=(b, seq // POOL_TM),
        in_specs=[
            pl.BlockSpec((None, POOL_TM, D_MODEL), lambda i, t: (i, t, 0)),
            pl.BlockSpec((1, D_MODEL), const2),
            pl.BlockSpec((D_MODEL, 2 * E_WIDTH), const2),
            pl.BlockSpec((len(POOL_WINDOWS), POOL_CH, POOL_CH), lambda i, t: (0, 0, 0)),
            pl.BlockSpec((1, E_WIDTH), const2),
            pl.BlockSpec((1, E_WIDTH), const2),
            pl.BlockSpec((E_WIDTH, D_MODEL), const2),
            pl.BlockSpec((1, D_MODEL), const2),
        ],
        out_specs=pl.BlockSpec((None, POOL_TM, D_MODEL), lambda i, t: (i, t, 0)),
        out_shape=jax.ShapeDtypeStruct(h.shape, F32),
        scratch_shapes=[
            pltpu.VMEM((POOL_HALO, E_WIDTH), F32),
            pltpu.VMEM((POOL_TM, E_WIDTH), BF16),
        ],
        compiler_params=pltpu.CompilerParams(
            dimension_semantics=("arbitrary", "arbitrary"),
            vmem_limit_bytes=VMEM_LIMIT_V7X),
        name="l1_pool",
    )(h, gpre, w_in, w_grp, b_grp, scale, w_out, gpost)


def kernel(x, positions, norm_pre, norm_post, attn_w_in, attn_w_out,
           pool_w_in, pool_w_grp, pool_b_grp, pool_scale, pool_w_out):
    b, seq, d = x.shape
    assert d == D_MODEL and seq % CHUNK == 0
    assert attn_w_in.shape[0] == 1 and pool_w_in.shape[0] == 1
    inv_freq = ROPE_THETA ** (-jnp.arange(0, ROT_DIM, 2, dtype=F32) / ROT_DIM)
    invf = jnp.tile(inv_freq, RES)[None, :]

    x3 = x.reshape(b, seq // RES, RES, D_MODEL)
    pos3 = positions.reshape(b, seq // RES, RES)
    xn, rope = _prenorm(x, pos3, norm_pre[0:1], invf)
    y = _attention(xn, rope, attn_w_in[0], _band_bias())
    h1 = _outproj(y, attn_w_out[0].astype(BF16), x3, norm_post[0:1]).reshape(b, seq, d)

    return _pool_layer(
        h1, norm_pre[1:2], pool_w_in[0].astype(BF16), pool_w_grp[0].astype(BF16),
        pool_b_grp[0].reshape(1, E_WIDTH), pool_scale[0:1], pool_w_out[0].astype(BF16), norm_post[1:2])
```

```python
import math

import numpy as np
import jax
import jax.numpy as jnp
from jax import lax
from jax.experimental import pallas as pl
from jax.experimental.pallas import tpu as pltpu

F32 = jnp.float32
BF16 = jnp.bfloat16

D_MODEL = 1024
E_WIDTH = 2048
HEAD_DIM = 128
N_HEADS = 16
ROT_DIM = 32
ROPE_THETA = 500000.0
RMS_EPS = 1e-6
NEG_INF = -1e30
POOL_WINDOWS = (2, 4, 8, 16)
POOL_CH = 512

RES = 16
MM = 128
CHUNK = RES * MM
QBLK = 128
HALO = (8, 32, 128)
PITCH = tuple(h + MM for h in HALO)
SCALE2 = math.log2(math.e) / math.sqrt(HEAD_DIM)
ATTN_UNROLL = 8
COMB_ROWS = 256

VMEM_LIMIT_V7X = 60000 * 1024


def _rms(x, g):
    ms = jnp.mean(x * x, axis=-1, keepdims=True)
    return x * lax.rsqrt(ms + RMS_EPS) * g


def _silu(z):
    hz = 0.5 * z
    return hz + hz * jnp.tanh(hz)


LANES = 128
N_SLABS = D_MODEL // LANES
ROPE_PARTNER = LANES // 2


def _rope_layout(w):
    half = ROT_DIM // 2
    rest = ROPE_PARTNER - half
    return jnp.concatenate(
        [w[:, :half], w[:, ROT_DIM:ROT_DIM + rest], w[:, half:ROT_DIM], w[:, ROT_DIM + rest:]], axis=1)


def _prenorm_kernel(*refs):
    x_slabs = refs[:N_SLABS]
    pos_ref, g_ref, invf_ref, xn_ref, rope_ref = refs[N_SLABS:]
    g = g_ref[...]
    half = ROT_DIM // 2
    wide = lax.broadcasted_iota(jnp.int32, (1, RES * half), 1)
    posf = pos_ref[...].astype(F32)
    pos_wide = jnp.zeros((MM, RES * half), F32)
    for r in range(RES):
        pos_wide = jnp.where(wide // half == r, posf[:, r:r + 1], pos_wide)
    ang = pos_wide * invf_ref[...]
    cos_all = jnp.cos(ang)
    sin_all = jnp.sin(ang)
    lane = lax.broadcasted_iota(jnp.int32, (1, LANES), 1)
    lo = lane < half
    hi = (lane >= ROPE_PARTNER) & (lane < ROPE_PARTNER + half)
    for r in range(RES):
        x = jnp.concatenate([s[pl.ds(r, MM, stride=RES), :] for s in x_slabs], axis=1)
        xn_ref[r * MM:(r + 1) * MM, :] = _rms(x, g).astype(BF16)
        tile = (r * half) // LANES
        off = (r * half) % LANES
        c_src = cos_all[:, tile * LANES:(tile + 1) * LANES]
        s_src = sin_all[:, tile * LANES:(tile + 1) * LANES]
        c_a = pltpu.roll(c_src, (LANES - off) % LANES, 1)
        c_b = pltpu.roll(c_src, (LANES - off + ROPE_PARTNER) % LANES, 1)
        s_a = pltpu.roll(s_src, (LANES - off) % LANES, 1)
        s_b = pltpu.roll(s_src, (LANES - off + ROPE_PARTNER) % LANES, 1)
        rows = slice(r * MM, (r + 1) * MM)
        rope_ref[rows, 0:LANES] = jnp.where(lo, c_a, jnp.where(hi, c_b, 1.0))
        rope_ref[rows, LANES:2 * LANES] = jnp.where(lo, -s_a, jnp.where(hi, s_b, 0.0))


def _prenorm(x, pos3, g, invf):
    b, seq, _ = x.shape
    n_chunks = seq // CHUNK
    return pl.pallas_call(
        _prenorm_kernel,
        grid=(b, n_chunks),
        in_specs=[
            *[pl.BlockSpec((None, CHUNK, LANES), lambda i, c, j=j: (i, c, j)) for j in range(N_SLABS)],
            pl.BlockSpec((None, MM, RES), lambda i, c: (i, c, 0)),
            pl.BlockSpec((1, D_MODEL), lambda i, c: (0, 0)),
            pl.BlockSpec((1, RES * ROT_DIM // 2), lambda i, c: (0, 0)),
        ],
        out_specs=[
            pl.BlockSpec((None, CHUNK, D_MODEL), lambda i, c: (i, c, 0)),
            pl.BlockSpec((None, CHUNK, 2 * LANES), lambda i, c: (i, c, 0)),
        ],
        out_shape=[
            jax.ShapeDtypeStruct((b, seq, D_MODEL), BF16),
            jax.ShapeDtypeStruct((b, seq, 2 * LANES), F32),
        ],
        compiler_params=pltpu.CompilerParams(
            dimension_semantics=("arbitrary", "arbitrary"),
            vmem_limit_bytes=VMEM_LIMIT_V7X),
        name="l0_prenorm",
    )(*([x] * N_SLABS), pos3, g, invf)


def _band_bias():
    qi = np.arange(QBLK)[:, None]
    kj = np.arange(2 * QBLK)[None, :]
    out = np.zeros((3, 2, QBLK, 2 * QBLK), np.float32)
    dist0 = 16 * ((qi % 8) - (kj % 16) + 8) + (qi // 8) - (kj // 16)
    halo0 = (kj % 16) < 8
    dist1 = 4 * ((qi % 32) - (kj % 64) + 32) + (qi // 32) - (kj // 64)
    halo1 = (kj % 64) < 32
    dist2 = qi - kj + 128
    halo2 = kj < 128
    for g, (dist, halo) in enumerate(((dist0, halo0), (dist1, halo1), (dist2, halo2))):
        band = (dist >= 0) & (dist <= QBLK)
        out[g, 0] = np.where(band, 0.0, NEG_INF)
        out[g, 1] = np.where(band & ~halo, 0.0, NEG_INF)
    return jnp.asarray(out)


def _attn_block(q, k, v, bias):
    s = lax.dot_general(q, k, (((1,), (1,)), ((), ())), preferred_element_type=F32) + bias
    m = jnp.max(s, axis=-1, keepdims=True)
    p = jnp.exp2(s - m).astype(BF16)
    va = jnp.concatenate([v, jnp.ones_like(v)], axis=1)
    od = jnp.dot(p, va, preferred_element_type=F32)
    return od[:, :HEAD_DIM], od[:, HEAD_DIM:], jnp.broadcast_to(m, (QBLK, HEAD_DIM))


def _attn_kernel(xn_ref, rope_ref, wq0, wk0, wv0, wz, wq1, wk1, wq2, wk2, wv1, wv2, bias_ref,
                 y_ref,
                 w_s, q0_s, k0_s, v0_s, q1_s, k1_s, v1_s, q2_s, k2_s, v2_s, z_s, acc_s, den_s, max_s):
    c = pl.program_id(2)
    kv_bufs = ((k0_s, v0_s), (k1_s, v1_s), (k2_s, v2_s))

    @pl.when(jnp.logical_and(pl.program_id(1) == 0, c == 0))
    def _():
        for j, w in enumerate((wq0, wk0, wv0, wz, wq1, wk1, wq2, wk2, wv1, wv2)):
            wj = _rope_layout(w[...]) if j in (0, 1, 4, 5, 6, 7) else w[...]
            w_s[:, j * HEAD_DIM:(j + 1) * HEAD_DIM] = wj.astype(BF16)

    @pl.when(c == 0)
    def _():
        for g, (kb, vb) in enumerate(kv_bufs):
            for r in range(RES):
                kb[r * PITCH[g]:r * PITCH[g] + HALO[g], :] = jnp.zeros((HALO[g], HEAD_DIM), kb.dtype)
                vb[r * PITCH[g]:r * PITCH[g] + HALO[g], :] = jnp.zeros((HALO[g], HEAD_DIM), vb.dtype)

    x = xn_ref[...]
    cos = rope_ref[:, 0:LANES]
    sin = rope_ref[:, LANES:2 * LANES]

    def rope(t):
        return t * cos + pltpu.roll(t, ROPE_PARTNER, 1) * sin

    def proj(tile):
        return jnp.dot(x, w_s[:, tile * 256:(tile + 1) * 256], preferred_element_type=F32)

    def put_cur(buf, g, val):
        for r in range(RES):
            buf[r * PITCH[g] + HALO[g]:(r + 1) * PITCH[g], :] = val[r * MM:(r + 1) * MM].astype(buf.dtype)

    t = proj(0)
    q0_s[...] = rope(t[:, :128]) * SCALE2
    put_cur(k0_s, 0, rope(t[:, 128:]))
    t = proj(1)
    put_cur(v0_s, 0, t[:, :128])
    z_s[...] = _silu(t[:, 128:])
    t = proj(2)
    q1_s[...] = (rope(t[:, :128]) * SCALE2).astype(BF16)
    put_cur(k1_s, 1, rope(t[:, 128:]))
    t = proj(3)
    q2_s[...] = (rope(t[:, :128]) * SCALE2).astype(BF16)
    put_cur(k2_s, 2, rope(t[:, 128:]))
    t = proj(4)
    put_cur(v1_s, 1, t[:, :128])
    put_cur(v2_s, 2, t[:, 128:])

    first = c == 0

    def body0(qb):
        off = qb * 8
        q = jnp.concatenate(
            [q0_s[pl.ds(pl.multiple_of(r * MM + off, 8), 8), :] for r in range(RES)], axis=0).astype(BF16)
        k = jnp.concatenate(
            [k0_s[pl.ds(pl.multiple_of(r * PITCH[0] + off, 8), 16), :] for r in range(RES)], axis=0).astype(BF16)
        v = jnp.concatenate(
            [v0_s[pl.ds(pl.multiple_of(r * PITCH[0] + off, 8), 16), :] for r in range(RES)], axis=0).astype(BF16)
        flag = jnp.logical_and(first, qb == 0).astype(jnp.int32)
        acc, den, mx = _attn_block(q, k, v, bias_ref[0, flag])
        for r in range(RES):
            rows = pl.ds(pl.multiple_of(r * MM + off, 8), 8)
            acc_s[0, rows, :] = acc[r * 8:(r + 1) * 8]
            den_s[0, rows, :] = den[r * 8:(r + 1) * 8]
            max_s[0, rows, :] = mx[r * 8:(r + 1) * 8]

    def body1(it):
        r4 = it // 4
        off = (it % 4) * 32
        q = jnp.concatenate(
            [q1_s[pl.ds(pl.multiple_of((r4 + 4 * cc) * MM + off, 32), 32), :] for cc in range(4)], axis=0)
        k = jnp.concatenate(
            [k1_s[pl.ds(pl.multiple_of((r4 + 4 * cc) * PITCH[1] + off, 32), 64), :] for cc in range(4)], axis=0)
        v = jnp.concatenate(
            [v1_s[pl.ds(pl.multiple_of((r4 + 4 * cc) * PITCH[1] + off, 32), 64), :] for cc in range(4)], axis=0)
        flag = jnp.logical_and(first, (it % 4) == 0).astype(jnp.int32)
        acc, den, mx = _attn_block(q, k, v, bias_ref[1, flag])
        for cc in range(4):
            rows = pl.ds(pl.multiple_of((r4 + 4 * cc) * MM + off, 32), 32)
            acc_s[1, rows, :] = acc[cc * 32:(cc + 1) * 32]
            den_s[1, rows, :] = den[cc * 32:(cc + 1) * 32]
            max_s[1, rows, :] = mx[cc * 32:(cc + 1) * 32]

    def body2(r):
        qrows = pl.ds(pl.multiple_of(r * MM, MM), MM)
        krows = pl.ds(pl.multiple_of(r * PITCH[2], PITCH[2]), PITCH[2])
        flag = first.astype(jnp.int32)
        acc2, den2, m2 = _attn_block(q2_s[qrows, :], k2_s[krows, :], v2_s[krows, :], bias_ref[2, flag])
        m0 = max_s[0, qrows, :]
        m1 = max_s[1, qrows, :]
        mmax = jnp.maximum(jnp.maximum(m0, m1), m2)
        e0 = jnp.exp2(m0 - mmax)
        e1 = jnp.exp2(m1 - mmax)
        e2 = jnp.exp2(m2 - mmax)
        num = e0 * acc_s[0, qrows, :] + e1 * acc_s[1, qrows, :] + e2 * acc2
        den = e0 * den_s[0, qrows, :] + e1 * den_s[1, qrows, :] + e2 * den2
        y_ref[qrows, :] = (num * (1.0 / den) * z_s[qrows, :]).astype(BF16)

    def all_blocks(_, carry):
        for i in range(RES):
            body0(i)
        for r4 in range(4):
            for qb in range(4):
                body1(r4 * 4 + qb)
            for cc in range(4):
                body2(r4 + 4 * cc)
        return carry

    lax.fori_loop(0, jnp.minimum(c + 1, 1), all_blocks, 0)

    for g, (kb, vb) in enumerate(kv_bufs):
        for r in range(RES):
            base = r * PITCH[g]
            kb[base:base + HALO[g], :] = kb[base + MM:base + MM + HALO[g], :]
            vb[base:base + HALO[g], :] = vb[base + MM:base + MM + HALO[g], :]


def _attention(xn, rope, w_in, bias):
    b, seq, _ = xn.shape
    n_chunks = seq // CHUNK

    def wspec(col_block):
        return pl.BlockSpec((D_MODEL, HEAD_DIM), lambda h, i, c: (0, col_block + h))

    col = lambda g, j: (g * 3 + j) * N_HEADS
    w_specs = [wspec(col(0, 0)), wspec(col(0, 1)), wspec(col(0, 2)), wspec(9 * N_HEADS),
               wspec(col(1, 0)), wspec(col(1, 1)), wspec(col(2, 0)), wspec(col(2, 1)),
               wspec(col(1, 2)), wspec(col(2, 2))]
    kv_rows = [RES * p for p in PITCH]
    return pl.pallas_call(
        _attn_kernel,
        grid=(N_HEADS, b, n_chunks),
        in_specs=[
            pl.BlockSpec((None, CHUNK, D_MODEL), lambda h, i, c: (i, c, 0)),
            pl.BlockSpec((None, CHUNK, 2 * LANES), lambda h, i, c: (i, c, 0)),
            *w_specs,
            pl.BlockSpec((3, 2, QBLK, 2 * QBLK), lambda h, i, c: (0, 0, 0, 0)),
        ],
        out_specs=pl.BlockSpec((None, CHUNK, HEAD_DIM), lambda h, i, c: (i, c, h)),
        out_shape=jax.ShapeDtypeStruct((b, seq, E_WIDTH), BF16),
        scratch_shapes=[
            pltpu.VMEM((D_MODEL, 10 * HEAD_DIM), BF16),
            pltpu.VMEM((CHUNK, HEAD_DIM), F32),
            pltpu.VMEM((kv_rows[0], HEAD_DIM), F32),
            pltpu.VMEM((kv_rows[0], HEAD_DIM), F32),
            pltpu.VMEM((CHUNK, HEAD_DIM), BF16),
            pltpu.VMEM((kv_rows[1], HEAD_DIM), BF16),
            pltpu.VMEM((kv_rows[1], HEAD_DIM), BF16),
            pltpu.VMEM((CHUNK, HEAD_DIM), BF16),
            pltpu.VMEM((kv_rows[2], HEAD_DIM), BF16),
            pltpu.VMEM((kv_rows[2], HEAD_DIM), BF16),
            pltpu.VMEM((CHUNK, HEAD_DIM), F32),
            pltpu.VMEM((2, CHUNK, HEAD_DIM), F32),
            pltpu.VMEM((2, CHUNK, HEAD_DIM), F32),
            pltpu.VMEM((2, CHUNK, HEAD_DIM), F32),
        ],
        compiler_params=pltpu.CompilerParams(
            dimension_semantics=("arbitrary", "arbitrary", "arbitrary"),
            vmem_limit_bytes=VMEM_LIMIT_V7X),
        name="l0_attention",
    )(xn, rope, *([w_in] * 10), bias)


OUT_RES = 8


def _outproj_kernel(y_ref, w_ref, x_ref, g_ref, o_ref):
    p = jnp.dot(y_ref[...], w_ref[...], preferred_element_type=F32)
    n = _rms(p, g_ref[...]).reshape(OUT_RES, MM, D_MODEL)
    o_ref[...] = x_ref[...] + jnp.swapaxes(n, 0, 1)


def _outproj(y, w_out, x3, g):
    b, seq, _ = y.shape
    n_chunks = seq // CHUNK
    steps = RES // OUT_RES
    return pl.pallas_call(
        _outproj_kernel,
        grid=(b, n_chunks, steps),
        in_specs=[
            pl.BlockSpec((None, OUT_RES * MM, E_WIDTH), lambda i, c, j: (i, c * steps + j, 0)),
            pl.BlockSpec((E_WIDTH, D_MODEL), lambda i, c, j: (0, 0)),
            pl.BlockSpec((None, MM, OUT_RES, D_MODEL), lambda i, c, j: (i, c, j, 0)),
            pl.BlockSpec((1, D_MODEL), lambda i, c, j: (0, 0)),
        ],
        out_specs=pl.BlockSpec((None, MM, OUT_RES, D_MODEL), lambda i, c, j: (i, c, j, 0)),
        out_shape=jax.ShapeDtypeStruct(x3.shape, F32),
        compiler_params=pltpu.CompilerParams(
            dimension_semantics=("arbitrary", "arbitrary", "arbitrary"),
            vmem_limit_bytes=VMEM_LIMIT_V7X),
        name="l0_outproj",
    )(y, w_out, x3, g)


POOL_TM = 1024
POOL_HALO = 16


def _pool_kernel(h_ref, gpre_ref, win_ref, wgrp_ref, bgrp_ref, scale_ref, wout_ref, gpost_ref,
                 o_ref, carry_s, y_s):
    t = pl.program_id(1)

    @pl.when(t == 0)
    def _():
        carry_s[...] = jnp.zeros(carry_s.shape, carry_s.dtype)

    h = h_ref[...]
    xn = _rms(h, gpre_ref[...]).astype(BF16)
    pos = lax.broadcasted_iota(jnp.int32, (POOL_TM, 1), 0) + t * POOL_TM
    for g, w in enumerate(POOL_WINDOWS):
        cols = slice(g * POOL_CH, (g + 1) * POOL_CH)
        u = jnp.dot(xn, win_ref[:, cols], preferred_element_type=F32)
        z = jnp.dot(xn, win_ref[:, E_WIDTH + g * POOL_CH:E_WIDTH + (g + 1) * POOL_CH],
                    preferred_element_type=F32)
        s = jnp.concatenate([carry_s[:, cols], u], axis=0)
        k = 1
        while k < w:
            s = s + pltpu.roll(s, k, 0)
            k *= 2
        carry_s[:, cols] = u[POOL_TM - POOL_HALO:]
        inv_cnt = 1.0 / jnp.minimum(pos + 1, w).astype(F32)
        pooled = s[POOL_HALO:] * inv_cnt - u
        hg = jnp.dot(pooled.astype(BF16), wgrp_ref[g], preferred_element_type=F32) + bgrp_ref[:, cols]
        y_s[:, cols] = (hg * scale_ref[:, cols] * _silu(z)).astype(BF16)
    p = jnp.dot(y_s[...], wout_ref[...], preferred_element_type=F32)
    o_ref[...] = h + _rms(p, gpost_ref[...])


def _pool_layer(h, gpre, w_in, w_grp, b_grp, scale, w_out, gpost):
    b, seq, _ = h.shape
    const2 = lambda i, t: (0, 0)
    return pl.pallas_call(
        _pool_kernel,
        grid=(b, seq // POOL_TM),
        in_specs=[
            pl.BlockSpec((None, POOL_TM, D_MODEL), lambda i, t: (i, t, 0)),
            pl.BlockSpec((1, D_MODEL), const2),
            pl.BlockSpec((D_MODEL, 2 * E_WIDTH), const2),
            pl.BlockSpec((len(POOL_WINDOWS), POOL_CH, POOL_CH), lambda i, t: (0, 0, 0)),
            pl.BlockSpec((1, E_WIDTH), const2),
            pl.BlockSpec((1, E_WIDTH), const2),
            pl.BlockSpec((E_WIDTH, D_MODEL), const2),
            pl.BlockSpec((1, D_MODEL), const2),
        ],
        out_specs=pl.BlockSpec((None, POOL_TM, D_MODEL), lambda i, t: (i, t, 0)),
        out_shape=jax.ShapeDtypeStruct(h.shape, F32),
        scratch_shapes=[
            pltpu.VMEM((POOL_HALO, E_WIDTH), F32),
            pltpu.VMEM((POOL_TM, E_WIDTH), BF16),
        ],
        compiler_params=pltpu.CompilerParams(
            dimension_semantics=("arbitrary", "arbitrary"),
            vmem_limit_bytes=VMEM_LIMIT_V7X),
        name="l1_pool",
    )(h, gpre, w_in, w_grp, b_grp, scale, w_out, gpost)


def kernel(x, positions, norm_pre, norm_post, attn_w_in, attn_w_out,
           pool_w_in, pool_w_grp, pool_b_grp, pool_scale, pool_w_out):
    b, seq, d = x.shape
    assert d == D_MODEL and seq % CHUNK == 0
    assert attn_w_in.shape[0] == 1 and pool_w_in.shape[0] == 1
    inv_freq = ROPE_THETA ** (-jnp.arange(0, ROT_DIM, 2, dtype=F32) / ROT_DIM)
    invf = jnp.tile(inv_freq, RES)[None, :]

    x3 = x.reshape(b, seq // RES, RES, D_MODEL)
    pos3 = positions.reshape(b, seq // RES, RES)
    xn, rope = _prenorm(x, pos3, norm_pre[0:1], invf)
    y = _attention(xn, rope, attn_w_in[0], _band_bias())
    h1 = _outproj(y, attn_w_out[0].astype(BF16), x3, norm_post[0:1]).reshape(b, seq, d)

    return _pool_layer(
        h1, norm_pre[1:2], pool_w_in[0].astype(BF16), pool_w_grp[0].astype(BF16),
        pool_b_grp[0].reshape(1, E_WIDTH), pool_scale[0:1], pool_w_out[0].astype(BF16), norm_post[1:2])
```

```python
import math

import numpy as np
import jax
import jax.numpy as jnp
from jax import lax
from jax.experimental import pallas as pl
from jax.experimental.pallas import tpu as pltpu

F32 = jnp.float32
BF16 = jnp.bfloat16

D_MODEL = 1024
E_WIDTH = 2048
HEAD_DIM = 128
N_HEADS = 16
ROT_DIM = 32
ROPE_THETA = 500000.0
RMS_EPS = 1e-6
NEG_INF = -1e30
POOL_WINDOWS = (2, 4, 8, 16)
POOL_CH = 512

RES = 16
MM = 128
CHUNK = RES * MM
QBLK = 128
HALO = (8, 32, 128)
PITCH = tuple(h + MM for h in HALO)
SCALE2 = math.log2(math.e) / math.sqrt(HEAD_DIM)
ATTN_UNROLL = 8
COMB_ROWS = 256

VMEM_LIMIT_V7X = 60000 * 1024


def _rms(x, g):
    ms = jnp.mean(x * x, axis=-1, keepdims=True)
    return x * lax.rsqrt(ms + RMS_EPS) * g


def _silu(z):
    hz = 0.5 * z
    return hz + hz * jnp.tanh(hz)


LANES = 128
ROPE_PARTNER = LANES // 2


def _rope_layout(w):
    half = ROT_DIM // 2
    rest = ROPE_PARTNER - half
    return jnp.concatenate(
        [w[:, :half], w[:, ROT_DIM:ROT_DIM + rest], w[:, half:ROT_DIM], w[:, ROT_DIM + rest:]], axis=1)


def _prenorm_kernel(x_ref, pos_ref, g_ref, invf_ref, xn_ref, rope_ref):
    xn = _rms(x_ref[...], g_ref[...]).reshape(MM, RES, D_MODEL)
    xn_ref[...] = jnp.swapaxes(xn, 0, 1).reshape(CHUNK, D_MODEL).astype(BF16)
    half = ROT_DIM // 2
    wide = lax.broadcasted_iota(jnp.int32, (1, RES * half), 1)
    posf = pos_ref[...].astype(F32)
    pos_wide = jnp.zeros((MM, RES * half), F32)
    for r in range(RES):
        pos_wide = jnp.where(wide // half == r, posf[:, r:r + 1], pos_wide)
    ang = pos_wide * invf_ref[...]
    cos_all = jnp.cos(ang)
    sin_all = jnp.sin(ang)
    lane = lax.broadcasted_iota(jnp.int32, (1, LANES), 1)
    lo = lane < half
    hi = (lane >= ROPE_PARTNER) & (lane < ROPE_PARTNER + half)
    for r in range(RES):
        tile = (r * half) // LANES
        off = (r * half) % LANES
        c_src = cos_all[:, tile * LANES:(tile + 1) * LANES]
        s_src = sin_all[:, tile * LANES:(tile + 1) * LANES]
        c_a = pltpu.roll(c_src, (LANES - off) % LANES, 1)
        c_b = pltpu.roll(c_src, (LANES - off + ROPE_PARTNER) % LANES, 1)
        s_a = pltpu.roll(s_src, (LANES - off) % LANES, 1)
        s_b = pltpu.roll(s_src, (LANES - off + ROPE_PARTNER) % LANES, 1)
        rows = slice(r * MM, (r + 1) * MM)
        rope_ref[rows, 0:LANES] = jnp.where(lo, c_a, jnp.where(hi, c_b, 1.0))
        rope_ref[rows, LANES:2 * LANES] = jnp.where(lo, -s_a, jnp.where(hi, s_b, 0.0))


def _prenorm(x, pos3, g, invf):
    b, seq, _ = x.shape
    n_chunks = seq // CHUNK
    return pl.pallas_call(
        _prenorm_kernel,
        grid=(b, n_chunks),
        in_specs=[
            pl.BlockSpec((None, CHUNK, D_MODEL), lambda i, c: (i, c, 0)),
            pl.BlockSpec((None, MM, RES), lambda i, c: (i, c, 0)),
            pl.BlockSpec((1, D_MODEL), lambda i, c: (0, 0)),
            pl.BlockSpec((1, RES * ROT_DIM // 2), lambda i, c: (0, 0)),
        ],
        out_specs=[
            pl.BlockSpec((None, CHUNK, D_MODEL), lambda i, c: (i, c, 0)),
            pl.BlockSpec((None, CHUNK, 2 * LANES), lambda i, c: (i, c, 0)),
        ],
        out_shape=[
            jax.ShapeDtypeStruct((b, seq, D_MODEL), BF16),
            jax.ShapeDtypeStruct((b, seq, 2 * LANES), F32),
        ],
        compiler_params=pltpu.CompilerParams(
            dimension_semantics=("arbitrary", "arbitrary"),
            vmem_limit_bytes=VMEM_LIMIT_V7X),
        name="l0_prenorm",
    )(x, pos3, g, invf)


def _band_bias():
    qi = np.arange(QBLK)[:, None]
    kj = np.arange(2 * QBLK)[None, :]
    out = np.zeros((3, 2, QBLK, 2 * QBLK), np.float32)
    dist0 = 16 * ((qi % 8) - (kj % 16) + 8) + (qi // 8) - (kj // 16)
    halo0 = (kj % 16) < 8
    dist1 = 4 * ((qi % 32) - (kj % 64) + 32) + (qi // 32) - (kj // 64)
    halo1 = (kj % 64) < 32
    dist2 = qi - kj + 128
    halo2 = kj < 128
    for g, (dist, halo) in enumerate(((dist0, halo0), (dist1, halo1), (dist2, halo2))):
        band = (dist >= 0) & (dist <= QBLK)
        out[g, 0] = np.where(band, 0.0, NEG_INF)
        out[g, 1] = np.where(band & ~halo, 0.0, NEG_INF)
    return jnp.asarray(out)


def _attn_block(q, k, v, bias):
    s = lax.dot_general(q, k, (((1,), (1,)), ((), ())), preferred_element_type=F32) + bias
    m = jnp.max(s, axis=-1, keepdims=True)
    p = jnp.exp2(s - m).astype(BF16)
    va = jnp.concatenate([v, jnp.ones_like(v)], axis=1)
    od = jnp.dot(p, va, preferred_element_type=F32)
    return od[:, :HEAD_DIM], od[:, HEAD_DIM:], jnp.broadcast_to(m, (QBLK, HEAD_DIM))


def _attn_kernel(xn_ref, rope_ref, wq0, wk0, wv0, wz, wq1, wk1, wq2, wk2, wv1, wv2, bias_ref,
                 y_ref,
                 w_s, q0_s, k0_s, v0_s, q1_s, k1_s, v1_s, q2_s, k2_s, v2_s, z_s, acc_s, den_s, max_s):
    c = pl.program_id(2)
    kv_bufs = ((k0_s, v0_s), (k1_s, v1_s), (k2_s, v2_s))

    @pl.when(jnp.logical_and(pl.program_id(1) == 0, c == 0))
    def _():
        for j, w in enumerate((wq0, wk0, wv0, wz, wq1, wk1, wq2, wk2, wv1, wv2)):
            wj = _rope_layout(w[...]) if j in (0, 1, 4, 5, 6, 7) else w[...]
            w_s[:, j * HEAD_DIM:(j + 1) * HEAD_DIM] = wj.astype(BF16)

    @pl.when(c == 0)
    def _():
        for g, (kb, vb) in enumerate(kv_bufs):
            for r in range(RES):
                kb[r * PITCH[g]:r * PITCH[g] + HALO[g], :] = jnp.zeros((HALO[g], HEAD_DIM), kb.dtype)
                vb[r * PITCH[g]:r * PITCH[g] + HALO[g], :] = jnp.zeros((HALO[g], HEAD_DIM), vb.dtype)

    x = xn_ref[...]
    cos = rope_ref[:, 0:LANES]
    sin = rope_ref[:, LANES:2 * LANES]

    def rope(t):
        return t * cos + pltpu.roll(t, ROPE_PARTNER, 1) * sin

    def proj(tile):
        return jnp.dot(x, w_s[:, tile * 256:(tile + 1) * 256], preferred_element_type=F32)

    def put_cur(buf, g, val):
        for r in range(RES):
            buf[r * PITCH[g] + HALO[g]:(r + 1) * PITCH[g], :] = val[r * MM:(r + 1) * MM].astype(buf.dtype)

    t = proj(0)
    q0_s[...] = rope(t[:, :128]) * SCALE2
    put_cur(k0_s, 0, rope(t[:, 128:]))
    t = proj(1)
    put_cur(v0_s, 0, t[:, :128])
    z_s[...] = _silu(t[:, 128:])
    t = proj(2)
    q1_s[...] = (rope(t[:, :128]) * SCALE2).astype(BF16)
    put_cur(k1_s, 1, rope(t[:, 128:]))
    t = proj(3)
    q2_s[...] = (rope(t[:, :128]) * SCALE2).astype(BF16)
    put_cur(k2_s, 2, rope(t[:, 128:]))
    t = proj(4)
    put_cur(v1_s, 1, t[:, :128])
    put_cur(v2_s, 2, t[:, 128:])

    first = c == 0

    def body0(qb):
        off = qb * 8
        q = jnp.concatenate(
            [q0_s[pl.ds(pl.multiple_of(r * MM + off, 8), 8), :] for r in range(RES)], axis=0).astype(BF16)
        k = jnp.concatenate(
            [k0_s[pl.ds(pl.multiple_of(r * PITCH[0] + off, 8), 16), :] for r in range(RES)], axis=0).astype(BF16)
        v = jnp.concatenate(
            [v0_s[pl.ds(pl.multiple_of(r * PITCH[0] + off, 8), 16), :] for r in range(RES)], axis=0).astype(BF16)
        flag = jnp.logical_and(first, qb == 0).astype(jnp.int32)
        acc, den, mx = _attn_block(q, k, v, bias_ref[0, flag])
        for r in range(RES):
            rows = pl.ds(pl.multiple_of(r * MM + off, 8), 8)
            acc_s[0, rows, :] = acc[r * 8:(r + 1) * 8]
            den_s[0, rows, :] = den[r * 8:(r + 1) * 8]
            max_s[0, rows, :] = mx[r * 8:(r + 1) * 8]

    def body1(it):
        r4 = it // 4
        off = (it % 4) * 32
        q = jnp.concatenate(
            [q1_s[pl.ds(pl.multiple_of((r4 + 4 * cc) * MM + off, 32), 32), :] for cc in range(4)], axis=0)
        k = jnp.concatenate(
            [k1_s[pl.ds(pl.multiple_of((r4 + 4 * cc) * PITCH[1] + off, 32), 64), :] for cc in range(4)], axis=0)
        v = jnp.concatenate(
            [v1_s[pl.ds(pl.multiple_of((r4 + 4 * cc) * PITCH[1] + off, 32), 64), :] for cc in range(4)], axis=0)
        flag = jnp.logical_and(first, (it % 4) == 0).astype(jnp.int32)
        acc, den, mx = _attn_block(q, k, v, bias_ref[1, flag])
        for cc in range(4):
            rows = pl.ds(pl.multiple_of((r4 + 4 * cc) * MM + off, 32), 32)
            acc_s[1, rows, :] = acc[cc * 32:(cc + 1) * 32]
            den_s[1, rows, :] = den[cc * 32:(cc + 1) * 32]
            max_s[1, rows, :] = mx[cc * 32:(cc + 1) * 32]

    def body2(r):
        qrows = pl.ds(pl.multiple_of(r * MM, MM), MM)
        krows = pl.ds(pl.multiple_of(r * PITCH[2], PITCH[2]), PITCH[2])
        flag = first.astype(jnp.int32)
        acc2, den2, m2 = _attn_block(q2_s[qrows, :], k2_s[krows, :], v2_s[krows, :], bias_ref[2, flag])
        m0 = max_s[0, qrows, :]
        m1 = max_s[1, qrows, :]
        mmax = jnp.maximum(jnp.maximum(m0, m1), m2)
        e0 = jnp.exp2(m0 - mmax)
        e1 = jnp.exp2(m1 - mmax)
        e2 = jnp.exp2(m2 - mmax)
        num = e0 * acc_s[0, qrows, :] + e1 * acc_s[1, qrows, :] + e2 * acc2
        den = e0 * den_s[0, qrows, :] + e1 * den_s[1, qrows, :] + e2 * den2
        y_ref[qrows, :] = (num * (1.0 / den) * z_s[qrows, :]).astype(BF16)

    def all_blocks(_, carry):
        for i in range(RES):
            body0(i)
        for r4 in range(4):
            for qb in range(4):
                body1(r4 * 4 + qb)
            for cc in range(4):
                body2(r4 + 4 * cc)
        return carry

    lax.fori_loop(0, jnp.minimum(c + 1, 1), all_blocks, 0)

    for g, (kb, vb) in enumerate(kv_bufs):
        for r in range(RES):
            base = r * PITCH[g]
            kb[base:base + HALO[g], :] = kb[base + MM:base + MM + HALO[g], :]
            vb[base:base + HALO[g], :] = vb[base + MM:base + MM + HALO[g], :]


def _attention(xn, rope, w_in, bias):
    b, seq, _ = xn.shape
    n_chunks = seq // CHUNK

    def wspec(col_block):
        return pl.BlockSpec((D_MODEL, HEAD_DIM), lambda h, i, c: (0, col_block + h))

    col = lambda g, j: (g * 3 + j) * N_HEADS
    w_specs = [wspec(col(0, 0)), wspec(col(0, 1)), wspec(col(0, 2)), wspec(9 * N_HEADS),
               wspec(col(1, 0)), wspec(col(1, 1)), wspec(col(2, 0)), wspec(col(2, 1)),
               wspec(col(1, 2)), wspec(col(2, 2))]
    kv_rows = [RES * p for p in PITCH]
    return pl.pallas_call(
        _attn_kernel,
        grid=(N_HEADS, b, n_chunks),
        in_specs=[
            pl.BlockSpec((None, CHUNK, D_MODEL), lambda h, i, c: (i, c, 0)),
            pl.BlockSpec((None, CHUNK, 2 * LANES), lambda h, i, c: (i, c, 0)),
            *w_specs,
            pl.BlockSpec((3, 2, QBLK, 2 * QBLK), lambda h, i, c: (0, 0, 0, 0)),
        ],
        out_specs=pl.BlockSpec((None, CHUNK, HEAD_DIM), lambda h, i, c: (i, c, h)),
        out_shape=jax.ShapeDtypeStruct((b, seq, E_WIDTH), BF16),
        scratch_shapes=[
            pltpu.VMEM((D_MODEL, 10 * HEAD_DIM), BF16),
            pltpu.VMEM((CHUNK, HEAD_DIM), F32),
            pltpu.VMEM((kv_rows[0], HEAD_DIM), F32),
            pltpu.VMEM((kv_rows[0], HEAD_DIM), F32),
            pltpu.VMEM((CHUNK, HEAD_DIM), BF16),
            pltpu.VMEM((kv_rows[1], HEAD_DIM), BF16),
            pltpu.VMEM((kv_rows[1], HEAD_DIM), BF16),
            pltpu.VMEM((CHUNK, HEAD_DIM), BF16),
            pltpu.VMEM((kv_rows[2], HEAD_DIM), BF16),
            pltpu.VMEM((kv_rows[2], HEAD_DIM), BF16),
            pltpu.VMEM((CHUNK, HEAD_DIM), F32),
            pltpu.VMEM((2, CHUNK, HEAD_DIM), F32),
            pltpu.VMEM((2, CHUNK, HEAD_DIM), F32),
            pltpu.VMEM((2, CHUNK, HEAD_DIM), F32),
        ],
        compiler_params=pltpu.CompilerParams(
            dimension_semantics=("arbitrary", "arbitrary", "arbitrary"),
            vmem_limit_bytes=VMEM_LIMIT_V7X),
        name="l0_attention",
    )(xn, rope, *([w_in] * 10), bias)


OUT_RES = 8


def _outproj_kernel(y_ref, w_ref, x_ref, g_ref, o_ref, w_s):
    @pl.when((pl.program_id(0) == 0) & (pl.program_id(1) == 0) & (pl.program_id(2) == 0))
    def _():
        w_s[...] = w_ref[...].astype(BF16)

    p = jnp.dot(y_ref[...], w_s[...], preferred_element_type=F32)
    n = _rms(p, g_ref[...]).reshape(OUT_RES, MM, D_MODEL)
    o_ref[...] = x_ref[...] + jnp.swapaxes(n, 0, 1)


def _outproj(y, w_out, x3, g):
    b, seq, _ = y.shape
    n_chunks = seq // CHUNK
    steps = RES // OUT_RES
    return pl.pallas_call(
        _outproj_kernel,
        grid=(b, n_chunks, steps),
        in_specs=[
            pl.BlockSpec((None, OUT_RES * MM, E_WIDTH), lambda i, c, j: (i, c * steps + j, 0)),
            pl.BlockSpec((E_WIDTH, D_MODEL), lambda i, c, j: (0, 0)),
            pl.BlockSpec((None, MM, OUT_RES, D_MODEL), lambda i, c, j: (i, c, j, 0)),
            pl.BlockSpec((1, D_MODEL), lambda i, c, j: (0, 0)),
        ],
        out_specs=pl.BlockSpec((None, MM, OUT_RES, D_MODEL), lambda i, c, j: (i, c, j, 0)),
        out_shape=jax.ShapeDtypeStruct(x3.shape, F32),
        scratch_shapes=[pltpu.VMEM((E_WIDTH, D_MODEL), BF16)],
        compiler_params=pltpu.CompilerParams(
            dimension_semantics=("arbitrary", "arbitrary", "arbitrary"),
            vmem_limit_bytes=VMEM_LIMIT_V7X),
        name="l0_outproj",
    )(y, w_out, x3, g)


POOL_TM = 1024
POOL_HALO = 16


def _pool_kernel(h_ref, gpre_ref, win_ref, wgrp_ref, bgrp_ref, scale_ref, wout_ref, gpost_ref,
                 o_ref, carry_s, y_s):
    t = pl.program_id(1)

    @pl.when(t == 0)
    def _():
        carry_s[...] = jnp.zeros(carry_s.shape, carry_s.dtype)

    h = h_ref[...]
    xn = _rms(h, gpre_ref[...]).astype(BF16)
    pos = lax.broadcasted_iota(jnp.int32, (POOL_TM, 1), 0) + t * POOL_TM
    for g, w in enumerate(POOL_WINDOWS):
        cols = slice(g * POOL_CH, (g + 1) * POOL_CH)
        u = jnp.dot(xn, win_ref[:, cols], preferred_element_type=F32)
        z = jnp.dot(xn, win_ref[:, E_WIDTH + g * POOL_CH:E_WIDTH + (g + 1) * POOL_CH],
                    preferred_element_type=F32)
        s = jnp.concatenate([carry_s[:, cols], u], axis=0)
        k = 1
        while k < w:
            s = s + pltpu.roll(s, k, 0)
            k *= 2
        carry_s[:, cols] = u[POOL_TM - POOL_HALO:]
        inv_cnt = 1.0 / jnp.minimum(pos + 1, w).astype(F32)
        pooled = s[POOL_HALO:] * inv_cnt - u
        hg = jnp.dot(pooled.astype(BF16), wgrp_ref[g], preferred_element_type=F32) + bgrp_ref[:, cols]
        y_s[:, cols] = (hg * scale_ref[:, cols] * _silu(z)).astype(BF16)
    p = jnp.dot(y_s[...], wout_ref[...], preferred_element_type=F32)
    o_ref[...] = h + _rms(p, gpost_ref[...])


def _pool_layer(h, gpre, w_in, w_grp, b_grp, scale, w_out, gpost):
    b, seq, _ = h.shape
    const2 = lambda i, t: (0, 0)
    return pl.pallas_call(
        _pool_kernel,
        grid=(b, seq // POOL_TM),
        in_specs=[
            pl.BlockSpec((None, POOL_TM, D_MODEL), lambda i, t: (i, t, 0)),
            pl.BlockSpec((1, D_MODEL), const2),
            pl.BlockSpec((D_MODEL, 2 * E_WIDTH), const2),
            pl.BlockSpec((len(POOL_WINDOWS), POOL_CH, POOL_CH), lambda i, t: (0, 0, 0)),
            pl.BlockSpec((1, E_WIDTH), const2),
            pl.BlockSpec((1, E_WIDTH), const2),
            pl.BlockSpec((E_WIDTH, D_MODEL), const2),
            pl.BlockSpec((1, D_MODEL), const2),
        ],
        out_specs=pl.BlockSpec((None, POOL_TM, D_MODEL), lambda i, t: (i, t, 0)),
        out_shape=jax.ShapeDtypeStruct(h.shape, F32),
        scratch_shapes=[
            pltpu.VMEM((POOL_HALO, E_WIDTH), F32),
            pltpu.VMEM((POOL_TM, E_WIDTH), BF16),
        ],
        compiler_params=pltpu.CompilerParams(
            dimension_semantics=("arbitrary", "arbitrary"),
            vmem_limit_bytes=VMEM_LIMIT_V7X),
        name="l1_pool",
    )(h, gpre, w_in, w_grp, b_grp, scale, w_out, gpost)


def kernel(x, positions, norm_pre, norm_post, attn_w_in, attn_w_out,
           pool_w_in, pool_w_grp, pool_b_grp, pool_scale, pool_w_out):
    b, seq, d = x.shape
    assert d == D_MODEL and seq % CHUNK == 0
    assert attn_w_in.shape[0] == 1 and pool_w_in.shape[0] == 1
    inv_freq = ROPE_THETA ** (-jnp.arange(0, ROT_DIM, 2, dtype=F32) / ROT_DIM)
    invf = jnp.tile(inv_freq, RES)[None, :]

    x3 = x.reshape(b, seq // RES, RES, D_MODEL)
    pos3 = positions.reshape(b, seq // RES, RES)
    xn, rope = _prenorm(x, pos3, norm_pre[0:1], invf)
    y = _attention(xn, rope, attn_w_in[0], _band_bias())
    h1 = _outproj(y, attn_w_out[0], x3, norm_post[0:1]).reshape(b, seq, d)

    return _pool_layer(
        h1, norm_pre[1:2], pool_w_in[0].astype(BF16), pool_w_grp[0].astype(BF16),
        pool_b_grp[0].reshape(1, E_WIDTH), pool_scale[0:1], pool_w_out[0].astype(BF16), norm_post[1:2])
```

```python
import math

import numpy as np
import jax
import jax.numpy as jnp
from jax import lax
from jax.experimental import pallas as pl
from jax.experimental.pallas import tpu as pltpu

F32 = jnp.float32
BF16 = jnp.bfloat16

D_MODEL = 1024
E_WIDTH = 2048
HEAD_DIM = 128
N_HEADS = 16
ROT_DIM = 32
ROPE_THETA = 500000.0
RMS_EPS = 1e-6
NEG_INF = -1e30
POOL_WINDOWS = (2, 4, 8, 16)
POOL_CH = 512

RES = 16
MM = 128
CHUNK = RES * MM
QBLK = 128
HALO = (8, 32, 128)
PITCH = tuple(h + MM for h in HALO)
SCALE2 = math.log2(math.e) / math.sqrt(HEAD_DIM)

MXU_COLS_V7X = 256
VMEM_LIMIT_V7X = 60000 * 1024


def _rms(x, g):
    ms = jnp.mean(x * x, axis=-1, keepdims=True)
    return x * lax.rsqrt(ms + RMS_EPS) * g


def _silu(z):
    hz = 0.5 * z
    return hz + hz * jnp.tanh(hz)


LANES = 128
ROPE_PARTNER = LANES // 2


def _rope_layout(w):
    half = ROT_DIM // 2
    rest = ROPE_PARTNER - half
    return jnp.concatenate(
        [w[:, :half], w[:, ROT_DIM:ROT_DIM + rest], w[:, half:ROT_DIM], w[:, ROT_DIM + rest:]], axis=1)


def _prenorm_kernel(x_ref, pos_ref, g_ref, invf_ref, xn_ref, rope_ref):
    xn = _rms(x_ref[...], g_ref[...]).reshape(MM, RES, D_MODEL)
    xn_ref[...] = jnp.swapaxes(xn, 0, 1).reshape(CHUNK, D_MODEL).astype(BF16)
    half = ROT_DIM // 2
    wide = lax.broadcasted_iota(jnp.int32, (1, RES * half), 1)
    posf = pos_ref[...].astype(F32)
    pos_wide = jnp.zeros((MM, RES * half), F32)
    for r in range(RES):
        pos_wide = jnp.where(wide // half == r, posf[:, r:r + 1], pos_wide)
    ang = pos_wide * invf_ref[...]
    cos_all = jnp.cos(ang)
    sin_all = jnp.sin(ang)
    lane = lax.broadcasted_iota(jnp.int32, (1, LANES), 1)
    lo = lane < half
    hi = (lane >= ROPE_PARTNER) & (lane < ROPE_PARTNER + half)
    for r in range(RES):
        tile = (r * half) // LANES
        off = (r * half) % LANES
        c_src = cos_all[:, tile * LANES:(tile + 1) * LANES]
        s_src = sin_all[:, tile * LANES:(tile + 1) * LANES]
        c_a = pltpu.roll(c_src, (LANES - off) % LANES, 1)
        c_b = pltpu.roll(c_src, (LANES - off + ROPE_PARTNER) % LANES, 1)
        s_a = pltpu.roll(s_src, (LANES - off) % LANES, 1)
        s_b = pltpu.roll(s_src, (LANES - off + ROPE_PARTNER) % LANES, 1)
        rows = slice(r * MM, (r + 1) * MM)
        rope_ref[rows, 0:LANES] = jnp.where(lo, c_a, jnp.where(hi, c_b, 1.0))
        rope_ref[rows, LANES:2 * LANES] = jnp.where(lo, -s_a, jnp.where(hi, s_b, 0.0))


def _prenorm(x, pos3, g, invf):
    b, seq, _ = x.shape
    n_chunks = seq // CHUNK
    return pl.pallas_call(
        _prenorm_kernel,
        grid=(b, n_chunks),
        in_specs=[
            pl.BlockSpec((None, CHUNK, D_MODEL), lambda i, c: (i, c, 0)),
            pl.BlockSpec((None, MM, RES), lambda i, c: (i, c, 0)),
            pl.BlockSpec((1, D_MODEL), lambda i, c: (0, 0)),
            pl.BlockSpec((1, RES * ROT_DIM // 2), lambda i, c: (0, 0)),
        ],
        out_specs=[
            pl.BlockSpec((None, CHUNK, D_MODEL), lambda i, c: (i, c, 0)),
            pl.BlockSpec((None, CHUNK, 2 * LANES), lambda i, c: (i, c, 0)),
        ],
        out_shape=[
            jax.ShapeDtypeStruct((b, seq, D_MODEL), BF16),
            jax.ShapeDtypeStruct((b, seq, 2 * LANES), F32),
        ],
        compiler_params=pltpu.CompilerParams(
            dimension_semantics=("arbitrary", "arbitrary"),
            vmem_limit_bytes=VMEM_LIMIT_V7X),
        name="l0_prenorm",
    )(x, pos3, g, invf)


def _band_bias():
    qi = np.arange(QBLK)[:, None]
    kj = np.arange(2 * QBLK)[None, :]
    out = np.zeros((3, 2, QBLK, 2 * QBLK), np.float32)
    dist0 = 16 * ((qi % 8) - (kj % 16) + 8) + (qi // 8) - (kj // 16)
    halo0 = (kj % 16) < 8
    dist1 = 4 * ((qi % 32) - (kj % 64) + 32) + (qi // 32) - (kj // 64)
    halo1 = (kj % 64) < 32
    dist2 = qi - kj + 128
    halo2 = kj < 128
    for g, (dist, halo) in enumerate(((dist0, halo0), (dist1, halo1), (dist2, halo2))):
        band = (dist >= 0) & (dist <= QBLK)
        out[g, 0] = np.where(band, 0.0, NEG_INF)
        out[g, 1] = np.where(band & ~halo, 0.0, NEG_INF)
    return jnp.asarray(out)


def _attn_block(q, k, v, bias):
    s = lax.dot_general(q, k, (((1,), (1,)), ((), ())), preferred_element_type=F32) + bias
    m = jnp.max(s, axis=-1, keepdims=True)
    p = jnp.exp2(s - m).astype(BF16)
    va = jnp.concatenate([v, jnp.ones_like(v)], axis=1)
    od = jnp.dot(p, va, preferred_element_type=F32)
    return od[:, :HEAD_DIM], od[:, HEAD_DIM:], jnp.broadcast_to(m, (QBLK, HEAD_DIM))


def _attn_kernel(xn_ref, rope_ref, wq0, wk0, wv0, wz, wq1, wk1, wq2, wk2, wv1, wv2, bias_ref,
                 y_ref,
                 w_s, q0_s, k0_s, v0_s, q1_s, k1_s, v1_s, q2_s, k2_s, v2_s, z_s, acc_s, den_s, max_s):
    c = pl.program_id(2)
    kv_bufs = ((k0_s, v0_s), (k1_s, v1_s), (k2_s, v2_s))

    @pl.when(jnp.logical_and(pl.program_id(1) == 0, c == 0))
    def _():
        for j, w in enumerate((wq0, wk0, wv0, wz, wq1, wk1, wq2, wk2, wv1, wv2)):
            wj = _rope_layout(w[...]) if j in (0, 1, 4, 5, 6, 7) else w[...]
            w_s[:, j * HEAD_DIM:(j + 1) * HEAD_DIM] = wj.astype(BF16)

    @pl.when(c == 0)
    def _():
        for g, (kb, vb) in enumerate(kv_bufs):
            for r in range(RES):
                kb[r * PITCH[g]:r * PITCH[g] + HALO[g], :] = jnp.zeros((HALO[g], HEAD_DIM), kb.dtype)
                vb[r * PITCH[g]:r * PITCH[g] + HALO[g], :] = jnp.zeros((HALO[g], HEAD_DIM), vb.dtype)

    x = xn_ref[...]
    cos = rope_ref[:, 0:LANES]
    sin = rope_ref[:, LANES:2 * LANES]

    def rope(t):
        return t * cos + pltpu.roll(t, ROPE_PARTNER, 1) * sin

    def proj(tile):
        return jnp.dot(x, w_s[:, tile * MXU_COLS_V7X:(tile + 1) * MXU_COLS_V7X], preferred_element_type=F32)

    def put_cur(buf, g, val):
        for r in range(RES):
            buf[r * PITCH[g] + HALO[g]:(r + 1) * PITCH[g], :] = val[r * MM:(r + 1) * MM].astype(buf.dtype)

    t = proj(0)
    q0_s[...] = rope(t[:, :128]) * SCALE2
    put_cur(k0_s, 0, rope(t[:, 128:]))
    t = proj(1)
    put_cur(v0_s, 0, t[:, :128])
    z_s[...] = _silu(t[:, 128:])
    t = proj(2)
    q1_s[...] = (rope(t[:, :128]) * SCALE2).astype(BF16)
    put_cur(k1_s, 1, rope(t[:, 128:]))
    t = proj(3)
    q2_s[...] = (rope(t[:, :128]) * SCALE2).astype(BF16)
    put_cur(k2_s, 2, rope(t[:, 128:]))
    t = proj(4)
    put_cur(v1_s, 1, t[:, :128])
    put_cur(v2_s, 2, t[:, 128:])

    first = c == 0

    def body0(qb):
        off = qb * 8
        q = jnp.concatenate(
            [q0_s[pl.ds(pl.multiple_of(r * MM + off, 8), 8), :] for r in range(RES)], axis=0).astype(BF16)
        k = jnp.concatenate(
            [k0_s[pl.ds(pl.multiple_of(r * PITCH[0] + off, 8), 16), :] for r in range(RES)], axis=0).astype(BF16)
        v = jnp.concatenate(
            [v0_s[pl.ds(pl.multiple_of(r * PITCH[0] + off, 8), 16), :] for r in range(RES)], axis=0).astype(BF16)
        flag = jnp.logical_and(first, qb == 0).astype(jnp.int32)
        acc, den, mx = _attn_block(q, k, v, bias_ref[0, flag])
        for r in range(RES):
            rows = pl.ds(pl.multiple_of(r * MM + off, 8), 8)
            acc_s[0, rows, :] = acc[r * 8:(r + 1) * 8]
            den_s[0, rows, :] = den[r * 8:(r + 1) * 8]
            max_s[0, rows, :] = mx[r * 8:(r + 1) * 8]

    def body1(it):
        r4 = it // 4
        off = (it % 4) * 32
        q = jnp.concatenate(
            [q1_s[pl.ds(pl.multiple_of((r4 + 4 * cc) * MM + off, 32), 32), :] for cc in range(4)], axis=0)
        k = jnp.concatenate(
            [k1_s[pl.ds(pl.multiple_of((r4 + 4 * cc) * PITCH[1] + off, 32), 64), :] for cc in range(4)], axis=0)
        v = jnp.concatenate(
            [v1_s[pl.ds(pl.multiple_of((r4 + 4 * cc) * PITCH[1] + off, 32), 64), :] for cc in range(4)], axis=0)
        flag = jnp.logical_and(first, (it % 4) == 0).astype(jnp.int32)
        acc, den, mx = _attn_block(q, k, v, bias_ref[1, flag])
        for cc in range(4):
            rows = pl.ds(pl.multiple_of((r4 + 4 * cc) * MM + off, 32), 32)
            acc_s[1, rows, :] = acc[cc * 32:(cc + 1) * 32]
            den_s[1, rows, :] = den[cc * 32:(cc + 1) * 32]
            max_s[1, rows, :] = mx[cc * 32:(cc + 1) * 32]

    def body2(r):
        qrows = pl.ds(pl.multiple_of(r * MM, MM), MM)
        krows = pl.ds(pl.multiple_of(r * PITCH[2], PITCH[2]), PITCH[2])
        flag = first.astype(jnp.int32)
        acc2, den2, m2 = _attn_block(q2_s[qrows, :], k2_s[krows, :], v2_s[krows, :], bias_ref[2, flag])
        m0 = max_s[0, qrows, :]
        m1 = max_s[1, qrows, :]
        mmax = jnp.maximum(jnp.maximum(m0, m1), m2)
        e0 = jnp.exp2(m0 - mmax)
        e1 = jnp.exp2(m1 - mmax)
        e2 = jnp.exp2(m2 - mmax)
        num = e0 * acc_s[0, qrows, :] + e1 * acc_s[1, qrows, :] + e2 * acc2
        den = e0 * den_s[0, qrows, :] + e1 * den_s[1, qrows, :] + e2 * den2
        y_ref[qrows, :] = (num * (1.0 / den) * z_s[qrows, :]).astype(BF16)

    def all_blocks(_, carry):
        for i in range(RES):
            body0(i)
        for r4 in range(4):
            for qb in range(4):
                body1(r4 * 4 + qb)
            for cc in range(4):
                body2(r4 + 4 * cc)
        return carry

    lax.fori_loop(0, jnp.minimum(c + 1, 1), all_blocks, 0)

    for g, (kb, vb) in enumerate(kv_bufs):
        for r in range(RES):
            base = r * PITCH[g]
            kb[base:base + HALO[g], :] = kb[base + MM:base + MM + HALO[g], :]
            vb[base:base + HALO[g], :] = vb[base + MM:base + MM + HALO[g], :]


def _attention(xn, rope, w_in, bias):
    b, seq, _ = xn.shape
    n_chunks = seq // CHUNK

    def wspec(col_block):
        return pl.BlockSpec((D_MODEL, HEAD_DIM), lambda h, i, c: (0, col_block + h))

    col = lambda g, j: (g * 3 + j) * N_HEADS
    w_specs = [wspec(col(0, 0)), wspec(col(0, 1)), wspec(col(0, 2)), wspec(9 * N_HEADS),
               wspec(col(1, 0)), wspec(col(1, 1)), wspec(col(2, 0)), wspec(col(2, 1)),
               wspec(col(1, 2)), wspec(col(2, 2))]
    kv_rows = [RES * p for p in PITCH]
    return pl.pallas_call(
        _attn_kernel,
        grid=(N_HEADS, b, n_chunks),
        in_specs=[
            pl.BlockSpec((None, CHUNK, D_MODEL), lambda h, i, c: (i, c, 0)),
            pl.BlockSpec((None, CHUNK, 2 * LANES), lambda h, i, c: (i, c, 0)),
            *w_specs,
            pl.BlockSpec((3, 2, QBLK, 2 * QBLK), lambda h, i, c: (0, 0, 0, 0)),
        ],
        out_specs=pl.BlockSpec((None, CHUNK, HEAD_DIM), lambda h, i, c: (i, c, h)),
        out_shape=jax.ShapeDtypeStruct((b, seq, E_WIDTH), BF16),
        scratch_shapes=[
            pltpu.VMEM((D_MODEL, 10 * HEAD_DIM), BF16),
            pltpu.VMEM((CHUNK, HEAD_DIM), F32),
            pltpu.VMEM((kv_rows[0], HEAD_DIM), F32),
            pltpu.VMEM((kv_rows[0], HEAD_DIM), F32),
            pltpu.VMEM((CHUNK, HEAD_DIM), BF16),
            pltpu.VMEM((kv_rows[1], HEAD_DIM), BF16),
            pltpu.VMEM((kv_rows[1], HEAD_DIM), BF16),
            pltpu.VMEM((CHUNK, HEAD_DIM), BF16),
            pltpu.VMEM((kv_rows[2], HEAD_DIM), BF16),
            pltpu.VMEM((kv_rows[2], HEAD_DIM), BF16),
            pltpu.VMEM((CHUNK, HEAD_DIM), F32),
            pltpu.VMEM((2, CHUNK, HEAD_DIM), F32),
            pltpu.VMEM((2, CHUNK, HEAD_DIM), F32),
            pltpu.VMEM((2, CHUNK, HEAD_DIM), F32),
        ],
        compiler_params=pltpu.CompilerParams(
            dimension_semantics=("arbitrary", "arbitrary", "arbitrary"),
            vmem_limit_bytes=VMEM_LIMIT_V7X),
        name="l0_attention",
    )(xn, rope, *([w_in] * 10), bias)


OUT_RES = 8


def _outproj_kernel(y_ref, w_ref, x_ref, g_ref, o_ref, w_s):
    @pl.when((pl.program_id(0) == 0) & (pl.program_id(1) == 0) & (pl.program_id(2) == 0))
    def _():
        w_s[...] = w_ref[...].astype(BF16)

    p = jnp.dot(y_ref[...], w_s[...], preferred_element_type=F32)
    n = _rms(p, g_ref[...]).reshape(OUT_RES, MM, D_MODEL)
    o_ref[...] = x_ref[...] + jnp.swapaxes(n, 0, 1)


def _outproj(y, w_out, x3, g):
    b, seq, _ = y.shape
    n_chunks = seq // CHUNK
    steps = RES // OUT_RES
    return pl.pallas_call(
        _outproj_kernel,
        grid=(b, n_chunks, steps),
        in_specs=[
            pl.BlockSpec((None, OUT_RES * MM, E_WIDTH), lambda i, c, j: (i, c * steps + j, 0)),
            pl.BlockSpec((E_WIDTH, D_MODEL), lambda i, c, j: (0, 0)),
            pl.BlockSpec((None, MM, OUT_RES, D_MODEL), lambda i, c, j: (i, c, j, 0)),
            pl.BlockSpec((1, D_MODEL), lambda i, c, j: (0, 0)),
        ],
        out_specs=pl.BlockSpec((None, MM, OUT_RES, D_MODEL), lambda i, c, j: (i, c, j, 0)),
        out_shape=jax.ShapeDtypeStruct(x3.shape, F32),
        scratch_shapes=[pltpu.VMEM((E_WIDTH, D_MODEL), BF16)],
        compiler_params=pltpu.CompilerParams(
            dimension_semantics=("arbitrary", "arbitrary", "arbitrary"),
            vmem_limit_bytes=VMEM_LIMIT_V7X),
        name="l0_outproj",
    )(y, w_out, x3, g)


POOL_TM = 1024
POOL_HALO = 16


def _pool_kernel(h_ref, gpre_ref, win_ref, wgrp_ref, bgrp_ref, scale_ref, wout_ref, gpost_ref,
                 o_ref, carry_s, y_s):
    t = pl.program_id(1)

    @pl.when(t == 0)
    def _():
        carry_s[...] = jnp.zeros(carry_s.shape, carry_s.dtype)

    h = h_ref[...]
    xn = _rms(h, gpre_ref[...]).astype(BF16)
    pos = lax.broadcasted_iota(jnp.int32, (POOL_TM, 1), 0) + t * POOL_TM
    for g, w in enumerate(POOL_WINDOWS):
        cols = slice(g * POOL_CH, (g + 1) * POOL_CH)
        u = jnp.dot(xn, win_ref[:, cols], preferred_element_type=F32)
        z = jnp.dot(xn, win_ref[:, E_WIDTH + g * POOL_CH:E_WIDTH + (g + 1) * POOL_CH],
                    preferred_element_type=F32)
        s = jnp.concatenate([carry_s[:, cols], u], axis=0)
        k = 1
        while k < w:
            s = s + pltpu.roll(s, k, 0)
            k *= 2
        carry_s[:, cols] = u[POOL_TM - POOL_HALO:]
        inv_cnt = 1.0 / jnp.minimum(pos + 1, w).astype(F32)
        pooled = s[POOL_HALO:] * inv_cnt - u
        hg = jnp.dot(pooled.astype(BF16), wgrp_ref[g], preferred_element_type=F32) + bgrp_ref[:, cols]
        y_s[:, cols] = (hg * scale_ref[:, cols] * _silu(z)).astype(BF16)
    p = jnp.dot(y_s[...], wout_ref[...], preferred_element_type=F32)
    o_ref[...] = h + _rms(p, gpost_ref[...])


def _pool_layer(h, gpre, w_in, w_grp, b_grp, scale, w_out, gpost):
    b, seq, _ = h.shape
    const2 = lambda i, t: (0, 0)
    return pl.pallas_call(
        _pool_kernel,
        grid=(b, seq // POOL_TM),
        in_specs=[
            pl.BlockSpec((None, POOL_TM, D_MODEL), lambda i, t: (i, t, 0)),
            pl.BlockSpec((1, D_MODEL), const2),
            pl.BlockSpec((D_MODEL, 2 * E_WIDTH), const2),
            pl.BlockSpec((len(POOL_WINDOWS), POOL_CH, POOL_CH), lambda i, t: (0, 0, 0)),
            pl.BlockSpec((1, E_WIDTH), const2),
            pl.BlockSpec((1, E_WIDTH), const2),
            pl.BlockSpec((E_WIDTH, D_MODEL), const2),
            pl.BlockSpec((1, D_MODEL), const2),
        ],
        out_specs=pl.BlockSpec((None, POOL_TM, D_MODEL), lambda i, t: (i, t, 0)),
        out_shape=jax.ShapeDtypeStruct(h.shape, F32),
        scratch_shapes=[
            pltpu.VMEM((POOL_HALO, E_WIDTH), F32),
            pltpu.VMEM((POOL_TM, E_WIDTH), BF16),
        ],
        compiler_params=pltpu.CompilerParams(
            dimension_semantics=("arbitrary", "arbitrary"),
            vmem_limit_bytes=VMEM_LIMIT_V7X),
        name="l1_pool",
    )(h, gpre, w_in, w_grp, b_grp, scale, w_out, gpost)


def kernel(x, positions, norm_pre, norm_post, attn_w_in, attn_w_out,
           pool_w_in, pool_w_grp, pool_b_grp, pool_scale, pool_w_out):
    b, seq, d = x.shape
    assert d == D_MODEL and seq % CHUNK == 0
    assert attn_w_in.shape[0] == 1 and pool_w_in.shape[0] == 1
    inv_freq = ROPE_THETA ** (-jnp.arange(0, ROT_DIM, 2, dtype=F32) / ROT_DIM)
    invf = jnp.tile(inv_freq, RES)[None, :]

    x3 = x.reshape(b, seq // RES, RES, D_MODEL)
    pos3 = positions.reshape(b, seq // RES, RES)
    xn, rope = _prenorm(x, pos3, norm_pre[0:1], invf)
    y = _attention(xn, rope, attn_w_in[0], _band_bias())
    h1 = _outproj(y, attn_w_out[0], x3, norm_post[0:1]).reshape(b, seq, d)

    return _pool_layer(
        h1, norm_pre[1:2], pool_w_in[0].astype(BF16), pool_w_grp[0].astype(BF16),
        pool_b_grp[0].reshape(1, E_WIDTH), pool_scale[0:1], pool_w_out[0].astype(BF16), norm_post[1:2])
```
